```python
import jax, jax.numpy as jnp
from jax import lax
import numpy as np

D_MODEL = 4096
BATCH = 4
SEQ = 2048
DEPTH = 2
DEC_BATCH = 128
DEC_SEQ = 1
PAST_LEN = 16384
PAGE_SIZE = 128

N_META = 16
A_HEADS = 4
A_HEAD_DIM = 256
A_WIDTH = A_HEADS * A_HEAD_DIM
A_CHUNK = 64
B_WIDTH = 1024
B_CONV = 31
C_HEADS = 16
C_NOPE = 128
C_ROPE = 64
C_QK = C_NOPE + C_ROPE
C_V = 128
C_Q_RANK = 768
C_KV_RANK = 256
C_ROW = C_KV_RANK + C_ROPE + C_HEADS
C_QBLOCK = 128
ROPE_THETA = 10000.0
D_FF = 8192
FFN_CONV = 3
EPS = 1e-6

IN_SPLITS = (C_Q_RANK, C_KV_RANK, C_ROPE,
             A_WIDTH, A_WIDTH, A_WIDTH, A_WIDTH, A_HEADS, A_HEADS,
             B_WIDTH, B_WIDTH,
             D_MODEL, D_MODEL, D_MODEL)
N_IN = sum(IN_SPLITS)
IN_OFFSETS = tuple(int(o) for o in np.cumsum(IN_SPLITS)[:-1])

kernel_name = 'hybrid_mlstm_conformer_mla_decoder_step'


def rms_norm(x, g):
    xf = x.astype(jnp.float32)
    y = xf * lax.rsqrt(jnp.mean(xf * xf, axis=-1, keepdims=True) + EPS)
    return (y * g.astype(jnp.float32)).astype(x.dtype)


def layer_norm(x, g, b):
    xf = x.astype(jnp.float32)
    mu = jnp.mean(xf, axis=-1, keepdims=True)
    var = jnp.mean(jnp.square(xf - mu), axis=-1, keepdims=True)
    return ((xf - mu) * lax.rsqrt(var + EPS) * g.astype(jnp.float32) + b.astype(jnp.float32)).astype(x.dtype)


def rope(x, pos):
    half = C_ROPE // 2
    inv_freq = ROPE_THETA ** (-jnp.arange(half, dtype=jnp.float32) / half)
    ang = pos.astype(jnp.float32)[:, None] * inv_freq[None, :]
    cos = jnp.cos(ang)[:, None, :]
    sin = jnp.sin(ang)[:, None, :]
    xf = x.astype(jnp.float32)
    x1, x2 = xf[..., :half], xf[..., half:]
    return jnp.concatenate([x1 * cos - x2 * sin, x1 * sin + x2 * cos], axis=-1).astype(x.dtype)


def causal_dwconv(x, hist, w):
    width = w.shape[0]
    xx = jnp.concatenate([hist.astype(x.dtype), x], axis=1)
    y = lax.conv_general_dilated(xx, w[:, None, :].astype(x.dtype), (1,), 'VALID',
                                 dimension_numbers=('NWC', 'WIO', 'NWC'),
                                 feature_group_count=x.shape[-1])
    return y, xx[:, xx.shape[1] - (width - 1):]


def mlstm_chunk(state, q, k, v, it, lf):
    c_prev, n_prev, m_prev = state
    L = q.shape[1]
    b = jnp.cumsum(lf, axis=1)
    causal = jnp.tril(jnp.ones((L, L), bool))[None, :, :, None]
    log_d = jnp.where(causal, b[:, :, None, :] - b[:, None, :, :] + it[:, None, :, :], -jnp.inf)
    log_inter = b + m_prev[:, None, :]
    m_t = jnp.maximum(log_inter, jnp.max(log_d, axis=2))
    d = jnp.exp(log_d - m_t[:, :, None, :])
    w_inter = jnp.exp(log_inter - m_t)
    s = jnp.einsum('bthd,bshd->btsh', q, k) * d
    num = w_inter[..., None] * jnp.einsum('bthd,bhde->bthe', q, c_prev) + jnp.einsum('btsh,bshe->bthe', s, v)
    den = w_inter * jnp.einsum('bthd,bhd->bth', q, n_prev) + jnp.sum(s, axis=2)
    h = num / jnp.maximum(jnp.abs(den), jnp.exp(-m_t))[..., None]
    m_new = m_t[:, -1]
    w_end = jnp.exp(b[:, -1:] - b + it - m_new[:, None])
    decay = jnp.exp(b[:, -1] + m_prev - m_new)
    c_new = decay[..., None, None] * c_prev + jnp.einsum('bsh,bshd,bshe->bhde', w_end, k, v)
    n_new = decay[..., None] * n_prev + jnp.einsum('bsh,bshd->bhd', w_end, k)
    return h, (c_new, n_new, m_new)


def mlstm_prompt(q, k, v, it, lf):
    nb = q.shape[0]
    st = (jnp.zeros((nb, A_HEADS, A_HEAD_DIM, A_HEAD_DIM), jnp.float32),
          jnp.zeros((nb, A_HEADS, A_HEAD_DIM), jnp.float32),
          jnp.zeros((nb, A_HEADS), jnp.float32))
    h0, st = mlstm_chunk(st, q[:, :N_META], k[:, :N_META], v[:, :N_META], it[:, :N_META], lf[:, :N_META])
    s_real = q.shape[1] - N_META
    nc = s_real // A_CHUNK

    def chunks(t):
        return jnp.swapaxes(t[:, N_META:].reshape((nb, nc, A_CHUNK) + t.shape[2:]), 0, 1)

    def body(carry, xs):
        h, carry = mlstm_chunk(carry, *xs)
        return carry, h

    st, hr = lax.scan(body, st, (chunks(q), chunks(k), chunks(v), chunks(it), chunks(lf)))
    hr = jnp.swapaxes(hr, 0, 1).reshape((nb, s_real, A_HEADS, A_HEAD_DIM))
    return jnp.concatenate([h0, hr], axis=1), st


def make_state_mlstm(c0, n0, m0):
    st = (c0.astype(jnp.float32), n0.astype(jnp.float32), m0.astype(jnp.float32))
    return lambda q, k, v, it, lf: mlstm_chunk(st, q, k, v, it, lf)


def mla_project(cq, ckv, kr, pos, lw):
    nb, s = cq.shape[:2]
    q = (rms_norm(cq, lw['g_cq']) @ lw['w_q_b']).reshape(nb, s, C_HEADS, C_QK)
    q = jnp.concatenate([q[..., :C_NOPE], rope(q[..., C_NOPE:], pos)], axis=-1)
    q = rms_norm(q, lw['g_qn']) * (lw['g_kn'] * C_QK ** -0.5)
    q_lat = jnp.einsum('bshn,chn->bshc', q[..., :C_NOPE], lw['w_uk'])
    q_rp = q[..., C_NOPE:]
    c = rms_norm(ckv, lw['g_ckv'])
    k_r = rope(kr[:, :, None, :], pos)[:, :, 0, :]
    k_nope = jnp.einsum('bsc,chn->bshn', c, lw['w_uk']).astype(jnp.float32)
    ms = (jnp.sum(k_nope * k_nope, axis=-1)
          + jnp.sum(jnp.square(k_r.astype(jnp.float32)), axis=-1, keepdims=True)) / C_QK
    kls = (-0.5 * jnp.log(ms + EPS)).astype(c.dtype)
    return q_lat, q_rp, jnp.concatenate([c, k_r, kls], axis=-1)


def mla_scores(q_lat, q_rp, rows):
    c = rows[..., :C_KV_RANK]
    k_r = rows[..., C_KV_RANK:C_KV_RANK + C_ROPE]
    kls = rows[..., C_KV_RANK + C_ROPE:]
    s = (jnp.einsum('bqhc,bkc->bhqk', q_lat, c) + jnp.einsum('bqhr,bkr->bhqk', q_rp, k_r)).astype(jnp.float32)
    return s * jnp.exp(jnp.swapaxes(kls.astype(jnp.float32), 1, 2))[:, :, None, :]


def attend_masked(q_lat, q_rp, qpos, rows):
    s = mla_scores(q_lat, q_rp, rows)
    kpos = jnp.arange(rows.shape[1])
    s = jnp.where(kpos[None, :] <= qpos[:, None], s, -jnp.inf)
    p = jax.nn.softmax(s, axis=-1)
    return jnp.einsum('bhqk,bkc->bqhc', p, rows[..., :C_KV_RANK].astype(jnp.float32))


def mla_attend_prompt(q_lat, q_rp, rows):
    nb = q_lat.shape[0]
    n_blk = (rows.shape[1] - N_META) // C_QBLOCK
    o_meta = attend_masked(q_lat[:, :N_META], q_rp[:, :N_META], jnp.arange(N_META), rows[:, :N_META])

    def blk(i):
        start = N_META + i * C_QBLOCK
        ql = lax.dynamic_slice_in_dim(q_lat, start, C_QBLOCK, axis=1)
        qr = lax.dynamic_slice_in_dim(q_rp, start, C_QBLOCK, axis=1)
        return attend_masked(ql, qr, start + jnp.arange(C_QBLOCK), rows)

    o = lax.map(blk, jnp.arange(n_blk))
    o = jnp.moveaxis(o, 0, 1).reshape((nb, n_blk * C_QBLOCK) + o.shape[3:])
    return jnp.concatenate([o_meta, o], axis=1)


def make_paged_attend(cache_mla, page_table, layer_idx):
    def attend(q_lat, q_rp, rows):
        t = rows.shape[1]
        s = mla_scores(q_lat, q_rp, rows)
        s = jnp.where(jnp.tril(jnp.ones((t, t), bool)), s, -jnp.inf)
        m = jnp.max(s, axis=-1)
        p = jnp.exp(s - m[..., None])
        carry = (m, jnp.sum(p, axis=-1),
                 jnp.einsum('bhqk,bkc->bhqc', p, rows[..., :C_KV_RANK].astype(jnp.float32)))

        def step(carry, pages):
            m, den, acc = carry
            prow = cache_mla[layer_idx, pages]
            s = mla_scores(q_lat, q_rp, prow)
            m_new = jnp.maximum(m, jnp.max(s, axis=-1))
            alpha = jnp.exp(m - m_new)
            p = jnp.exp(s - m_new[..., None])
            acc = alpha[..., None] * acc + jnp.einsum('bhqk,bkc->bhqc', p, prow[..., :C_KV_RANK].astype(jnp.float32))
            return (m_new, alpha * den + jnp.sum(p, axis=-1), acc), None

        (m, den, acc), _ = lax.scan(step, carry, page_table.T)
        return jnp.swapaxes(acc / den[..., None], 1, 2)
    return attend


def layer(x, pos, lw, attend, mlstm_run, conv_hist, ffn_hist):
    nb, s, _ = x.shape
    xn = rms_norm(x, lw['g_mix'])
    (cq, ckv, kr, aq, ak, av, ao, ai, af, bu, bg, ga, gb, gc) = jnp.split(xn @ lw['w_in'], IN_OFFSETS, axis=-1)

    heads = lambda t: t.reshape(nb, s, A_HEADS, A_HEAD_DIM).astype(jnp.float32)
    i_pre = (ai + lw['b_i']).astype(jnp.float32)
    log_f = jax.nn.log_sigmoid((af + lw['b_f']).astype(jnp.float32))
    h_a, st_a = mlstm_run(heads(aq), heads(ak) * A_HEAD_DIM ** -0.5, heads(av), i_pre, log_f)
    h_a = rms_norm(h_a, lw['g_a_norm']).reshape(nb, s, A_WIDTH).astype(x.dtype)
    y_a = (h_a * jax.nn.sigmoid(ao)) @ lw['w_a_out']

    u = bu * jax.nn.sigmoid(bg)
    u, conv_new = causal_dwconv(u, conv_hist, lw['w_b_conv'])
    u = layer_norm(u + lw['b_b_conv'], lw['g_b_ln'], lw['b_b_ln'])
    y_b = jax.nn.silu(u) @ lw['w_b_out']

    q_lat, q_rp, rows = mla_project(cq, ckv, kr, pos, lw)
    o_lat = attend(q_lat, q_rp, rows)
    o = jnp.einsum('bshc,chv->bshv', o_lat, lw['w_uv']).reshape(nb, s, C_HEADS * C_V).astype(x.dtype)
    y_c = o @ lw['w_c_out']

    mixed = jax.nn.sigmoid(ga) * y_a + jax.nn.sigmoid(gb) * y_b + jax.nn.sigmoid(gc) * y_c
    x = x + mixed @ lw['w_out']

    hup = rms_norm(x, lw['g_ffn']) @ lw['w_up']
    hup_c, ffn_new = causal_dwconv(hup, ffn_hist, lw['w_ffn_conv'])
    gate, up = jnp.split(hup_c, 2, axis=-1)
    x = x + (jax.nn.silu(gate) * up) @ lw['w_down']
    return x, rows, st_a, conv_new, ffn_new


def setup_inputs(seed: int = 0) -> dict:
    key = jax.random.key(seed)
    keys = jax.random.split(key, 48)
    ctr = [0]

    def nxt():
        ctr[0] += 1
        return keys[ctr[0] - 1]

    def nrm(shape, scale=1.0):
        return jax.random.normal(nxt(), shape, jnp.float32) * scale

    def gain(shape):
        return 1.0 + 0.01 * jax.random.normal(nxt(), shape, jnp.float32)

    n_pages = PAST_LEN // PAGE_SIZE
    n_used = DEC_BATCH * n_pages
    n_pool = n_used + max(1, n_used // 4)
    page_table = jax.random.permutation(nxt(), n_pool)[:n_used].reshape(DEC_BATCH, n_pages).astype(jnp.int32)

    inp = {}
    inp['x_prompt'] = nrm((BATCH, SEQ, D_MODEL))
    inp['x_sample'] = nrm((DEC_BATCH, DEC_SEQ, D_MODEL))
    inp['cache_mla'] = nrm((DEPTH, n_pool, PAGE_SIZE, C_ROW))
    inp['page_table'] = page_table
    inp['state_mlstm_c'] = nrm((DEPTH, DEC_BATCH, A_HEADS, A_HEAD_DIM, A_HEAD_DIM))
    inp['state_mlstm_n'] = nrm((DEPTH, DEC_BATCH, A_HEADS, A_HEAD_DIM))
    inp['state_mlstm_m'] = nrm((DEPTH, DEC_BATCH, A_HEADS))
    inp['state_conv'] = nrm((DEPTH, DEC_BATCH, B_CONV - 1, B_WIDTH))
    inp['state_ffn_conv'] = nrm((DEPTH, DEC_BATCH, FFN_CONV - 1, 2 * D_FF))
    inp['meta_tokens'] = nrm((N_META, D_MODEL))
    inp['g_mix'] = gain((DEPTH, D_MODEL))
    inp['w_in'] = nrm((DEPTH, D_MODEL, N_IN), D_MODEL ** -0.5)
    inp['b_i'] = nrm((DEPTH, A_HEADS), 0.1)
    inp['b_f'] = 3.0 + 3.0 * jnp.linspace(0.0, 1.0, A_HEADS)[None, :] + nrm((DEPTH, A_HEADS), 0.1)
    inp['g_a_norm'] = gain((DEPTH, A_HEADS, A_HEAD_DIM))
    inp['w_a_out'] = nrm((DEPTH, A_WIDTH, D_MODEL), A_WIDTH ** -0.5)
    inp['w_b_conv'] = nrm((DEPTH, B_CONV, B_WIDTH), B_CONV ** -0.5)
    inp['b_b_conv'] = nrm((DEPTH, B_WIDTH), 0.01)
    inp['g_b_ln'] = gain((DEPTH, B_WIDTH))
    inp['b_b_ln'] = nrm((DEPTH, B_WIDTH), 0.01)
    inp['w_b_out'] = nrm((DEPTH, B_WIDTH, D_MODEL), B_WIDTH ** -0.5)
    inp['g_cq'] = gain((DEPTH, C_Q_RANK))
    inp['w_q_b'] = nrm((DEPTH, C_Q_RANK, C_HEADS * C_QK), C_Q_RANK ** -0.5)
    inp['g_ckv'] = gain((DEPTH, C_KV_RANK))
    inp['w_uk'] = nrm((DEPTH, C_KV_RANK, C_HEADS, C_NOPE), C_KV_RANK ** -0.5)
    inp['w_uv'] = nrm((DEPTH, C_KV_RANK, C_HEADS, C_V), C_KV_RANK ** -0.5)
    inp['g_qn'] = gain((DEPTH, C_QK))
    inp['g_kn'] = gain((DEPTH, C_QK))
    inp['w_c_out'] = nrm((DEPTH, C_HEADS * C_V, D_MODEL), (C_HEADS * C_V) ** -0.5)
    inp['w_out'] = nrm((DEPTH, D_MODEL, D_MODEL), D_MODEL ** -0.5)
    inp['g_ffn'] = gain((DEPTH, D_MODEL))
    inp['w_up'] = nrm((DEPTH, D_MODEL, 2 * D_FF), D_MODEL ** -0.5)
    inp['w_ffn_conv'] = nrm((DEPTH, FFN_CONV, 2 * D_FF), FFN_CONV ** -0.5)
    inp['w_down'] = nrm((DEPTH, D_FF, D_MODEL), D_FF ** -0.5)
    return inp


def reference(x_prompt, x_sample, cache_mla, page_table, state_mlstm_c, state_mlstm_n, state_mlstm_m,
              state_conv, state_ffn_conv, meta_tokens, g_mix, w_in, b_i, b_f, g_a_norm, w_a_out,
              w_b_conv, b_b_conv, g_b_ln, b_b_ln, w_b_out, g_cq, w_q_b, g_ckv, w_uk, w_uv, g_qn, g_kn,
              w_c_out, w_out, g_ffn, w_up, w_ffn_conv, w_down):
    nb = x_prompt.shape[0]
    dt = x_prompt.dtype
    xp = jnp.concatenate([jnp.broadcast_to(meta_tokens[None].astype(dt), (nb, N_META, D_MODEL)), x_prompt], axis=1)
    xs = x_sample
    pos_p = jnp.arange(xp.shape[1])
    pos_s = PAST_LEN + jnp.arange(xs.shape[1])
    rows_p, rows_s, c_p, c_s, n_p, n_s, m_p, m_s = [], [], [], [], [], [], [], []
    conv_p, conv_s, ffn_p, ffn_s = [], [], [], []
    for l in range(DEPTH):
        lw = dict(g_mix=g_mix[l], w_in=w_in[l], b_i=b_i[l], b_f=b_f[l], g_a_norm=g_a_norm[l],
                  w_a_out=w_a_out[l], w_b_conv=w_b_conv[l], b_b_conv=b_b_conv[l], g_b_ln=g_b_ln[l],
                  b_b_ln=b_b_ln[l], w_b_out=w_b_out[l], g_cq=g_cq[l], w_q_b=w_q_b[l], g_ckv=g_ckv[l],
                  w_uk=w_uk[l], w_uv=w_uv[l], g_qn=g_qn[l], g_kn=g_kn[l], w_c_out=w_c_out[l],
                  w_out=w_out[l], g_ffn=g_ffn[l], w_up=w_up[l], w_ffn_conv=w_ffn_conv[l], w_down=w_down[l])
        xp, r, st, cv, fv = layer(xp, pos_p, lw, mla_attend_prompt, mlstm_prompt,
                                  jnp.zeros((nb, B_CONV - 1, B_WIDTH), dt),
                                  jnp.zeros((nb, FFN_CONV - 1, 2 * D_FF), dt))
        rows_p.append(r); c_p.append(st[0]); n_p.append(st[1]); m_p.append(st[2])
        conv_p.append(cv); ffn_p.append(fv)
        xs, r, st, cv, fv = layer(xs, pos_s, lw, make_paged_attend(cache_mla, page_table, l),
                                  make_state_mlstm(state_mlstm_c[l], state_mlstm_n[l], state_mlstm_m[l]),
                                  state_conv[l], state_ffn_conv[l])
        rows_s.append(r); c_s.append(st[0]); n_s.append(st[1]); m_s.append(st[2])
        conv_s.append(cv); ffn_s.append(fv)
    y_prompt = xp[:, N_META:]
    y_sample = xs
    sd = x_sample.dtype
    return (y_prompt, y_sample,
            jnp.stack(rows_p), jnp.stack(rows_s),
            jnp.stack(c_p).astype(dt), jnp.stack(c_s).astype(sd),
            jnp.stack(n_p).astype(dt), jnp.stack(n_s).astype(sd),
            jnp.stack(m_p).astype(dt), jnp.stack(m_s).astype(sd),
            jnp.stack(conv_p), jnp.stack(conv_s),
            jnp.stack(ffn_p), jnp.stack(ffn_s))
```

```python
import functools

import jax
import jax.numpy as jnp
import numpy as np
from jax import lax
from jax.experimental import pallas as pl
from jax.experimental.pallas import tpu as pltpu

F32 = jnp.float32
BF16 = jnp.bfloat16

D_MODEL = 4096
BATCH = 4
SEQ = 2048
DEPTH = 2
DEC_BATCH = 128
PAST_LEN = 16384
PAGE_SIZE = 128
N_PAGES = PAST_LEN // PAGE_SIZE
N_META = 16
A_HEADS = 4
A_HEAD_DIM = 256
A_WIDTH = A_HEADS * A_HEAD_DIM
A_CHUNK = 64
B_WIDTH = 1024
B_CONV = 31
C_HEADS = 16
C_NOPE = 128
C_ROPE = 64
C_QK = C_NOPE + C_ROPE
C_V = 128
C_Q_RANK = 768
C_KV_RANK = 256
C_ROW = C_KV_RANK + C_ROPE + C_HEADS
ROPE_THETA = 10000.0
D_FF = 8192
FFN_CONV = 3
EPS = 1e-6

N_REAL = BATCH * SEQ
META0 = N_REAL
N_PROMPT = N_REAL + BATCH * N_META
SAMP0 = 8320
MP = SAMP0 + DEC_BATCH

Y_AQ, Y_AK, Y_AV, Y_AO = 0, 1024, 2048, 3072
Y_BU, Y_BG = 4096, 5120
Y_GA, Y_GB, Y_GC = 6144, 10240, 14336
Y_CQ, Y_CKV, Y_SMALL = 18432, 19200, 19456
NY = 19968
LANE_AI, LANE_AF = 64, 68

QH = 256
KW = 384

VMEM_LIMIT = 56 * 1024 * 1024
PAGE_GROUP = 8


def _params(*sem):
    return pltpu.CompilerParams(dimension_semantics=sem, vmem_limit_bytes=VMEM_LIMIT)


def _mm_body(*refs, nk, mode, has_prev):
    x_ref, w_ref = refs[0], refs[1]
    idx = 2
    r_ref = g_ref = p_ref = None
    if mode == "resid":
        r_ref = refs[idx]; idx += 1
    if mode == "gate":
        g_ref = refs[idx]; idx += 1
        if has_prev:
            p_ref = refs[idx]; idx += 1
    o_ref = refs[idx]; idx += 1
    acc_ref = refs[idx] if nk > 1 else None

    def epilogue(acc):
        if mode == "resid":
            acc = r_ref[...] + acc
        elif mode == "gate":
            acc = jax.nn.sigmoid(g_ref[...]) * acc
            if p_ref is not None:
                acc = p_ref[...] + acc
        o_ref[...] = acc.astype(o_ref.dtype)

    part = jnp.dot(x_ref[...], w_ref[...], preferred_element_type=F32)
    if nk == 1:
        epilogue(part)
    else:
        k = pl.program_id(2)

        @pl.when(k == 0)
        def _():
            acc_ref[...] = part

        @pl.when(jnp.logical_and(k > 0, k < nk - 1))
        def _():
            acc_ref[...] += part

        @pl.when(k == nk - 1)
        def _():
            epilogue(acc_ref[...] + part)


def _matmul(x, w, *, tm, tn, tk=None, out_dtype=F32, resid=None, gate=None, gate_off=0, prev=None, name):
    m, kdim = x.shape
    n = w.shape[1]
    tk = kdim if tk is None else tk
    nk = kdim // tk
    assert m % tm == 0 and n % tn == 0 and kdim % tk == 0 and gate_off % tn == 0
    mode = "resid" if resid is not None else ("gate" if gate is not None else "plain")
    in_specs = [pl.BlockSpec((tm, tk), lambda i, j, k: (i, k)),
                pl.BlockSpec((tk, tn), lambda i, j, k: (k, j))]
    args = [x, w]
    if resid is not None:
        in_specs.append(pl.BlockSpec((tm, tn), lambda i, j, k: (i, j)))
        args.append(resid)
    if gate is not None:
        goff = gate_off // tn
        in_specs.append(pl.BlockSpec((tm, tn), lambda i, j, k: (i, goff + j)))
        args.append(gate)
        if prev is not None:
            in_specs.append(pl.BlockSpec((tm, tn), lambda i, j, k: (i, j)))
            args.append(prev)
    return pl.pallas_call(
        functools.partial(_mm_body, nk=nk, mode=mode, has_prev=prev is not None),
        out_shape=jax.ShapeDtypeStruct((m, n), out_dtype),
        grid=(m // tm, n // tn, nk),
        in_specs=in_specs,
        out_specs=pl.BlockSpec((tm, tn), lambda i, j, k: (i, j)),
        scratch_shapes=[pltpu.VMEM((tm, tn), F32)] if nk > 1 else [],
        compiler_params=_params("parallel", "parallel", "arbitrary"),
        name=name,
    )(*args)


def _rms_body(x_ref, g_ref, o_ref):
    x = x_ref[...]
    ms = jnp.mean(x * x, axis=-1, keepdims=True)
    o_ref[...] = (x * lax.rsqrt(ms + EPS) * g_ref[...]).astype(o_ref.dtype)


def _rms_norm(x, g, *, width, col_off, tm, name):
    m = x.shape[0]
    cblk = col_off // width
    return pl.pallas_call(
        _rms_body,
        out_shape=jax.ShapeDtypeStruct((m, width), BF16),
        grid=(m // tm,),
        in_specs=[pl.BlockSpec((tm, width), lambda i: (i, cblk)),
                  pl.BlockSpec((1, width), lambda i: (0, 0))],
        out_specs=pl.BlockSpec((tm, width), lambda i: (i, 0)),
        compiler_params=_params("parallel"),
        name=name,
    )(x, g.reshape(1, width))


def _lane_pick(g, lane_idx):
    lane = lax.broadcasted_iota(jnp.int32, g.shape, 1)
    return jnp.sum(jnp.where(lane == lane_idx, g, 0.0), axis=1, keepdims=True)


def _head_norm_gate(hh, gn, ao):
    ms = jnp.mean(hh * hh, axis=-1, keepdims=True)
    return (hh * lax.rsqrt(ms + EPS) * gn * jax.nn.sigmoid(ao)).astype(BF16)


def _mlstm_prompt_body(bias_ref, qm_ref, km_ref, vm_ref, aom_ref, gm_ref, irm_ref, frm_ref,
                       q_ref, k_ref, v_ref, ao_ref, g_ref, irr_ref, frr_ref, gn_ref,
                       hm_ref, h_ref, c_ref, n_ref, m_ref):
    h = pl.program_id(1)
    b_i = bias_ref[h]
    b_f = bias_ref[A_HEADS + h]
    gn = gn_ref[...]

    def chunk(length, q, k, v, ao, g, ai_row, af_row, n_prev, m_prev):
        k = k * (A_HEAD_DIM ** -0.5)
        it_col = _lane_pick(g, LANE_AI + h) + b_i
        lf_col = jax.nn.log_sigmoid(_lane_pick(g, LANE_AF + h) + b_f)
        it_row = ai_row + b_i
        lf_row = jax.nn.log_sigmoid(af_row + b_f)
        t_i = lax.broadcasted_iota(jnp.int32, (length, length), 0)
        s_i = lax.broadcasted_iota(jnp.int32, (length, length), 1)
        causal = s_i <= t_i
        b_col = jnp.sum(jnp.where(causal, lf_row, 0.0), axis=1, keepdims=True)
        b_row = jnp.sum(jnp.where(t_i <= s_i, lf_col, 0.0), axis=0, keepdims=True)
        log_d = jnp.where(causal, b_col - b_row + it_row, -jnp.inf)
        log_inter = b_col + m_prev
        m_t = jnp.maximum(log_inter, jnp.max(log_d, axis=1, keepdims=True))
        d = jnp.exp(log_d - m_t)
        w_inter = jnp.exp(log_inter - m_t)
        qb = q.astype(BF16)
        kb = k.astype(BF16)
        vb = v.astype(BF16)
        c_prev = c_ref[...]
        s = lax.dot_general(qb, kb, (((1,), (1,)), ((), ())), preferred_element_type=F32) * d
        num = (w_inter * jnp.dot(qb, c_prev.astype(BF16), preferred_element_type=F32)
               + jnp.dot(s.astype(BF16), vb, preferred_element_type=F32))
        den = w_inter * jnp.sum(q * n_prev, axis=1, keepdims=True) + jnp.sum(s, axis=1, keepdims=True)
        hh = num / jnp.maximum(jnp.abs(den), jnp.exp(-m_t))
        m_new = m_t[length - 1:length, :]
        b_last = b_col[length - 1:length, :]
        w_end = jnp.exp(b_last - b_col + it_col - m_new)
        decay = jnp.exp(b_last + m_prev - m_new)
        kw = k * w_end
        c_ref[...] = decay * c_prev + lax.dot_general(
            kw.astype(BF16), vb, (((0,), (0,)), ((), ())), preferred_element_type=F32)
        n_new = decay * n_prev + jnp.sum(kw, axis=0, keepdims=True)
        return _head_norm_gate(hh, gn, ao), n_new, m_new

    c_ref[...] = jnp.zeros_like(c_ref)
    n0 = jnp.zeros((1, A_HEAD_DIM), F32)
    m0 = jnp.zeros((1, 1), F32)
    out, n1, m1 = chunk(N_META, qm_ref[...], km_ref[...], vm_ref[...], aom_ref[...], gm_ref[...],
                        irm_ref[...], frm_ref[...], n0, m0)
    hm_ref[...] = out

    def step(i, carry):
        n_prev, m_prev = carry
        r0 = pl.multiple_of(i * A_CHUNK, A_CHUNK)
        rows = pl.ds(r0, A_CHUNK)
        out, n_new, m_new = chunk(A_CHUNK, q_ref[rows, :], k_ref[rows, :], v_ref[rows, :], ao_ref[rows, :],
                                  g_ref[rows, :], irr_ref[pl.ds(i, 1), :], frr_ref[pl.ds(i, 1), :],
                                  n_prev, m_prev)
        h_ref[rows, :] = out
        return n_new, m_new

    n_fin, m_fin = lax.fori_loop(0, SEQ // A_CHUNK, step, (n1, m1))
    n_ref[...] = n_fin
    m_ref[...] = m_fin


def _mlstm_prompt(y, gt_meta, gt_real, bias, gnorm):
    meta_blk = META0 // N_META

    def ycol(off):
        return off // A_HEAD_DIM

    def meta_spec(off):
        return pl.BlockSpec((N_META, A_HEAD_DIM), lambda b, h: (meta_blk + b, ycol(off) + h))

    def real_spec(off):
        return pl.BlockSpec((SEQ, A_HEAD_DIM), lambda b, h: (b, ycol(off) + h))

    in_specs = [
        pl.BlockSpec(memory_space=pltpu.SMEM),
        meta_spec(Y_AQ), meta_spec(Y_AK), meta_spec(Y_AV), meta_spec(Y_AO),
        pl.BlockSpec((N_META, 128), lambda b, h: (meta_blk + b, Y_SMALL // 128)),
        pl.BlockSpec((None, None, 1, N_META), lambda b, h: (b, h, 0, 0)),
        pl.BlockSpec((None, None, 1, N_META), lambda b, h: (b, A_HEADS + h, 0, 0)),
        real_spec(Y_AQ), real_spec(Y_AK), real_spec(Y_AV), real_spec(Y_AO),
        pl.BlockSpec((SEQ, 128), lambda b, h: (b, Y_SMALL // 128)),
        pl.BlockSpec((None, None, SEQ // A_CHUNK, A_CHUNK), lambda b, h: (b, h, 0, 0)),
        pl.BlockSpec((None, None, SEQ // A_CHUNK, A_CHUNK), lambda b, h: (b, A_HEADS + h, 0, 0)),
        pl.BlockSpec((None, 1, A_HEAD_DIM), lambda b, h: (h, 0, 0)),
    ]
    out_shape = [
        jax.ShapeDtypeStruct((BATCH * N_META, A_WIDTH), BF16),
        jax.ShapeDtypeStruct((N_REAL, A_WIDTH), BF16),
        jax.ShapeDtypeStruct((BATCH, A_HEADS, A_HEAD_DIM, A_HEAD_DIM), F32),
        jax.ShapeDtypeStruct((BATCH, A_HEADS, 1, A_HEAD_DIM), F32),
        jax.ShapeDtypeStruct((BATCH, A_HEADS, 1, 1), F32),
    ]
    out_specs = [
        pl.BlockSpec((N_META, A_HEAD_DIM), lambda b, h: (b, h)),
        pl.BlockSpec((SEQ, A_HEAD_DIM), lambda b, h: (b, h)),
        pl.BlockSpec((None, None, A_HEAD_DIM, A_HEAD_DIM), lambda b, h: (b, h, 0, 0)),
        pl.BlockSpec((None, None, 1, A_HEAD_DIM), lambda b, h: (b, h, 0, 0)),
        pl.BlockSpec((None, None, 1, 1), lambda b, h: (b, h, 0, 0)),
    ]
    return pl.pallas_call(
        _mlstm_prompt_body,
        out_shape=out_shape,
        grid=(BATCH, A_HEADS),
        in_specs=in_specs,
        out_specs=out_specs,
        compiler_params=_params("parallel", "parallel"),
        name="mlstm_prompt",
    )(bias, y, y, y, y, y, gt_meta, gt_meta, y, y, y, y, y, gt_real, gt_real, gnorm)


MS_NB = 8


def _mlstm_sample_body(bias_ref, q_ref, k_ref, v_ref, ao_ref, g_ref, kt_ref, qt_ref, c_ref, n_ref, m_ref, gn_ref,
                       h_ref, co_ref, no_ref, mo_ref):
    h = pl.program_id(0)
    bt = pl.program_id(1)
    b_i = bias_ref[h]
    b_f = bias_ref[A_HEADS + h]
    q = q_ref[...]
    k = k_ref[...] * (A_HEAD_DIM ** -0.5)
    v = v_ref[...]
    g = g_ref[...]
    it = _lane_pick(g, LANE_AI + h) + b_i
    lf = jax.nn.log_sigmoid(_lane_pick(g, LANE_AF + h) + b_f)
    m_prev = m_ref[...]
    n_prev = n_ref[...]
    log_inter = lf + m_prev
    m_t = jnp.maximum(log_inter, it)
    d = jnp.exp(it - m_t)
    w_inter = jnp.exp(log_inter - m_t)
    s = jnp.sum(q * k, axis=1, keepdims=True) * d
    den = w_inter * jnp.sum(q * n_prev, axis=1, keepdims=True) + s
    wv = d * v
    lane = lax.broadcasted_iota(jnp.int32, (A_HEAD_DIM, DEC_BATCH), 1)
    kt = kt_ref[...] * (A_HEAD_DIM ** -0.5)
    qt = qt_ref[...]
    qc_rows = []
    for j in range(MS_NB):
        sel = lane == bt * MS_NB + j
        k_col = jnp.sum(jnp.where(sel, kt, 0.0), axis=1, keepdims=True)
        q_col = jnp.sum(jnp.where(sel, qt, 0.0), axis=1, keepdims=True)
        c_prev = c_ref[j]
        qc_rows.append(jnp.sum(q_col * c_prev, axis=0, keepdims=True))
        co_ref[j] = w_inter[j:j + 1, :] * c_prev + k_col * wv[j:j + 1, :]
    qc = jnp.concatenate(qc_rows, axis=0)
    num = w_inter * qc + s * v
    hh = num / jnp.maximum(jnp.abs(den), jnp.exp(-m_t))
    h_ref[...] = _head_norm_gate(hh, gn_ref[...], ao_ref[...])
    no_ref[...] = w_inter * n_prev + d * k
    mo_ref[...] = m_t


def _mlstm_sample(y, kt, qt, c_state, n_state_t, m_state_t, bias, gnorm):
    rb0 = SAMP0 // MS_NB

    def yspec(off):
        return pl.BlockSpec((MS_NB, A_HEAD_DIM), lambda h, bt: (rb0 + bt, off // A_HEAD_DIM + h))

    in_specs = [
        pl.BlockSpec(memory_space=pltpu.SMEM),
        yspec(Y_AQ), yspec(Y_AK), yspec(Y_AV), yspec(Y_AO),
        pl.BlockSpec((MS_NB, 128), lambda h, bt: (rb0 + bt, Y_SMALL // 128)),
        pl.BlockSpec((None, A_HEAD_DIM, DEC_BATCH), lambda h, bt: (h, 0, 0)),
        pl.BlockSpec((None, A_HEAD_DIM, DEC_BATCH), lambda h, bt: (h, 0, 0)),
        pl.BlockSpec((MS_NB, None, A_HEAD_DIM, A_HEAD_DIM), lambda h, bt: (bt, h, 0, 0)),
        pl.BlockSpec((None, MS_NB, A_HEAD_DIM), lambda h, bt: (h, bt, 0)),
        pl.BlockSpec((None, MS_NB, 1), lambda h, bt: (h, bt, 0)),
        pl.BlockSpec((None, 1, A_HEAD_DIM), lambda h, bt: (h, 0, 0)),
    ]
    out_shape = [
        jax.ShapeDtypeStruct((DEC_BATCH, A_WIDTH), BF16),
        jax.ShapeDtypeStruct((DEC_BATCH, A_HEADS, A_HEAD_DIM, A_HEAD_DIM), F32),
        jax.ShapeDtypeStruct((A_HEADS, DEC_BATCH, A_HEAD_DIM), F32),
        jax.ShapeDtypeStruct((A_HEADS, DEC_BATCH, 1), F32),
    ]
    out_specs = [
        pl.BlockSpec((MS_NB, A_HEAD_DIM), lambda h, bt: (bt, h)),
        pl.BlockSpec((MS_NB, None, A_HEAD_DIM, A_HEAD_DIM), lambda h, bt: (bt, h, 0, 0)),
        pl.BlockSpec((None, MS_NB, A_HEAD_DIM), lambda h, bt: (h, bt, 0)),
        pl.BlockSpec((None, MS_NB, 1), lambda h, bt: (h, bt, 0)),
    ]
    return pl.pallas_call(
        _mlstm_sample_body,
        out_shape=out_shape,
        grid=(A_HEADS, DEC_BATCH // MS_NB),
        in_specs=in_specs,
        out_specs=out_specs,
        compiler_params=_params("parallel", "parallel"),
        name="mlstm_sample",
    )(bias, y, y, y, y, y, kt, qt, c_state, n_state_t, m_state_t, gnorm)


CONV_PAD = 32
CONV_ROWS = 32


def _ln_swish(u, bias, g, b):
    u = u + bias
    mu = jnp.mean(u, axis=-1, keepdims=True)
    var = jnp.mean(jnp.square(u - mu), axis=-1, keepdims=True)
    y = (u - mu) * lax.rsqrt(var + EPS) * g + b
    return (y * jax.nn.sigmoid(y)).astype(BF16)


def _conv_prompt_body(bum_ref, bgm_ref, bu_ref, bg_ref, w_ref, cb_ref, lg_ref, lb_ref,
                      hm_ref, h_ref, st_ref, u_s):
    seq0 = CONV_PAD + N_META
    u_s[0:CONV_PAD, :] = jnp.zeros((CONV_PAD, B_WIDTH), F32)
    u_s[CONV_PAD:seq0, :] = bum_ref[...] * jax.nn.sigmoid(bgm_ref[...])

    def glu(i, _):
        rows = pl.ds(pl.multiple_of(i * 256, 256), 256)
        u_s[pl.ds(pl.multiple_of(seq0 + i * 256, 16), 256), :] = bu_ref[rows, :] * jax.nn.sigmoid(bg_ref[rows, :])
        return 0

    lax.fori_loop(0, SEQ // 256, glu, 0)
    bias, lg, lb = cb_ref[...], lg_ref[...], lb_ref[...]

    def conv_rows(start, nrows):
        win = u_s[pl.ds(start, nrows + CONV_PAD), :]
        acc = jnp.zeros((nrows, B_WIDTH), F32)
        for b in range(8):
            taps = range(b, B_CONV, 8)
            shifted = win[2 + b:2 + b + nrows + taps[-1] - b, :]
            for k in taps:
                acc = acc + w_ref[k:k + 1, :] * shifted[k - b:k - b + nrows, :]
        return _ln_swish(acc, bias, lg, lb)

    hm_ref[...] = conv_rows(0, N_META)

    def step(i, _):
        r0 = pl.multiple_of(i * CONV_ROWS, CONV_ROWS)
        h_ref[pl.ds(r0, CONV_ROWS), :] = conv_rows(pl.multiple_of(N_META + r0, 8), CONV_ROWS)
        return 0

    lax.fori_loop(0, SEQ // CONV_ROWS, step, 0)
    st_ref[...] = u_s[CONV_PAD + N_META + SEQ - (B_CONV - 1):CONV_PAD + N_META + SEQ, :]


def _conv_prompt(y, w, cb, lg, lb):
    meta_blk = META0 // N_META
    cu, cg = Y_BU // B_WIDTH, Y_BG // B_WIDTH
    vec = pl.BlockSpec((1, B_WIDTH), lambda b: (0, 0))
    return pl.pallas_call(
        _conv_prompt_body,
        out_shape=[jax.ShapeDtypeStruct((BATCH * N_META, B_WIDTH), BF16),
                   jax.ShapeDtypeStruct((N_REAL, B_WIDTH), BF16),
                   jax.ShapeDtypeStruct((BATCH, B_CONV - 1, B_WIDTH), F32)],
        grid=(BATCH,),
        in_specs=[pl.BlockSpec((N_META, B_WIDTH), lambda b: (meta_blk + b, cu)),
                  pl.BlockSpec((N_META, B_WIDTH), lambda b: (meta_blk + b, cg)),
                  pl.BlockSpec((SEQ, B_WIDTH), lambda b: (b, cu)),
                  pl.BlockSpec((SEQ, B_WIDTH), lambda b: (b, cg)),
                  pl.BlockSpec((B_CONV, B_WIDTH), lambda b: (0, 0)),
                  vec, vec, vec],
        out_specs=[pl.BlockSpec((N_META, B_WIDTH), lambda b: (b, 0)),
                   pl.BlockSpec((SEQ, B_WIDTH), lambda b: (b, 0)),
                   pl.BlockSpec((None, B_CONV - 1, B_WIDTH), lambda b: (b, 0, 0))],
        scratch_shapes=[pltpu.VMEM((CONV_PAD + N_META + SEQ, B_WIDTH), F32)],
        compiler_params=_params("parallel"),
        name="conv_prompt",
    )(y, y, y, y, w, cb, lg, lb)


CS_NB = 8


def _conv_sample_body(bu_ref, bg_ref, st_ref, w_ref, cb_ref, lg_ref, lb_ref, h_ref, so_ref):
    u = bu_ref[...] * jax.nn.sigmoid(bg_ref[...])
    w_hist = w_ref[0:B_CONV - 1, :]
    hist_rows = []
    for j in range(CS_NB):
        st = st_ref[j]
        hist_rows.append(jnp.sum(st * w_hist, axis=0, keepdims=True))
        so_ref[j, 0:B_CONV - 2, :] = st[1:B_CONV - 1, :]
        so_ref[j, B_CONV - 2:B_CONV - 1, :] = u[j:j + 1, :]
    acc = jnp.concatenate(hist_rows, axis=0) + w_ref[B_CONV - 1:B_CONV, :] * u
    h_ref[...] = _ln_swish(acc, cb_ref[...], lg_ref[...], lb_ref[...])


def _conv_sample(y, state, w, cb, lg, lb):
    rb0 = SAMP0 // CS_NB
    vec = pl.BlockSpec((1, B_WIDTH), lambda i: (0, 0))
    return pl.pallas_call(
        _conv_sample_body,
        out_shape=[jax.ShapeDtypeStruct((DEC_BATCH, B_WIDTH), BF16),
                   jax.ShapeDtypeStruct((DEC_BATCH, B_CONV - 1, B_WIDTH), F32)],
        grid=(DEC_BATCH // CS_NB,),
        in_specs=[pl.BlockSpec((CS_NB, B_WIDTH), lambda i: (rb0 + i, Y_BU // B_WIDTH)),
                  pl.BlockSpec((CS_NB, B_WIDTH), lambda i: (rb0 + i, Y_BG // B_WIDTH)),
                  pl.BlockSpec((CS_NB, B_CONV - 1, B_WIDTH), lambda i: (i, 0, 0)),
                  pl.BlockSpec((B_CONV, B_WIDTH), lambda i: (0, 0)),
                  vec, vec, vec],
        out_specs=[pl.BlockSpec((CS_NB, B_WIDTH), lambda i: (i, 0)),
                   pl.BlockSpec((CS_NB, B_CONV - 1, B_WIDTH), lambda i: (i, 0, 0))],
        compiler_params=_params("parallel"),
        name="conv_sample",
    )(y, y, state, w, cb, lg, lb)


def _rope_rotate(x, cos_t, sin_t):
    lane = lax.broadcasted_iota(jnp.int32, x.shape, 1)
    swapped = jnp.where(lane < C_ROPE // 2, pltpu.roll(x, 128 - C_ROPE // 2, 1), pltpu.roll(x, C_ROPE // 2, 1))
    return x * cos_t + swapped * sin_t


def _qpost_body(q_ref, cos_ref, sin_ref, gn_ref, gr_ref, wuk_ref, o_ref):
    cos_t, sin_t = cos_ref[...], sin_ref[...]
    gn, gr = gn_ref[...], gr_ref[...]
    for h in range(C_HEADS):
        nope = q_ref[:, h * QH:h * QH + C_NOPE]
        rot = _rope_rotate(q_ref[:, h * QH + C_NOPE:(h + 1) * QH], cos_t, sin_t)
        ms = (jnp.sum(nope * nope, axis=1, keepdims=True) + jnp.sum(rot * rot, axis=1, keepdims=True)) / C_QK
        r = lax.rsqrt(ms + EPS)
        q_lat = jnp.dot((nope * r * gn).astype(BF16), wuk_ref[h], preferred_element_type=F32)
        o_ref[h, :, 0:C_KV_RANK] = q_lat.astype(BF16)
        o_ref[h, :, C_KV_RANK:KW] = (rot * r * gr).astype(BF16)


def _qpost(q, cos_t, sin_t, gain_n, gain_r, wuk_t, *, tm):
    return pl.pallas_call(
        _qpost_body,
        out_shape=jax.ShapeDtypeStruct((C_HEADS, MP, KW), BF16),
        grid=(MP // tm,),
        in_specs=[pl.BlockSpec((tm, C_HEADS * QH), lambda i: (i, 0)),
                  pl.BlockSpec((tm, 128), lambda i: (i, 0)),
                  pl.BlockSpec((tm, 128), lambda i: (i, 0)),
                  pl.BlockSpec((1, 128), lambda i: (0, 0)),
                  pl.BlockSpec((1, 128), lambda i: (0, 0)),
                  pl.BlockSpec((C_HEADS, C_NOPE, C_KV_RANK), lambda i: (0, 0, 0))],
        out_specs=pl.BlockSpec((C_HEADS, tm, KW), lambda i: (0, i, 0)),
        compiler_params=_params("parallel"),
        name="mla_qpost",
    )(q, cos_t, sin_t, gain_n, gain_r, wuk_t)


def _rows_body(ckv_ref, sm_ref, cos_ref, sin_ref, g_ref, wuk_ref, rows_ref, kb_ref, e_ref):
    x = ckv_ref[...]
    c = x * lax.rsqrt(jnp.mean(x * x, axis=-1, keepdims=True) + EPS) * g_ref[...]
    k_r = _rope_rotate(sm_ref[...], cos_ref[...], sin_ref[...])
    k_nope = jnp.dot(c.astype(BF16), wuk_ref[...], preferred_element_type=F32)
    kr_ss = jnp.sum(k_r * k_r, axis=1, keepdims=True)
    lane = lax.broadcasted_iota(jnp.int32, k_r.shape, 1)
    kls = jnp.zeros(k_r.shape, F32)
    for h in range(C_HEADS):
        kn = k_nope[:, h * C_NOPE:(h + 1) * C_NOPE]
        ms = (jnp.sum(kn * kn, axis=1, keepdims=True) + kr_ss) / C_QK
        kls = jnp.where(lane == h, -0.5 * jnp.log(ms + EPS), kls)
    rows_ref[:, 0:C_KV_RANK] = c
    tail = k_r + pltpu.roll(kls, C_ROPE, 1)
    rows_ref[:, C_KV_RANK:C_ROW] = tail[:, 0:C_ROW - C_KV_RANK]
    kb_ref[:, 0:C_KV_RANK] = c.astype(BF16)
    kb_ref[:, C_KV_RANK:KW] = k_r.astype(BF16)
    e_ref[...] = jnp.where(lane < C_HEADS, jnp.exp(kls), 0.0)


def _rows(y, cos_t, sin_t, g_ckv, wuk2, *, tm):
    return pl.pallas_call(
        _rows_body,
        out_shape=[jax.ShapeDtypeStruct((MP, C_ROW), F32),
                   jax.ShapeDtypeStruct((MP, KW), BF16),
                   jax.ShapeDtypeStruct((MP, 128), F32)],
        grid=(MP // tm,),
        in_specs=[pl.BlockSpec((tm, C_KV_RANK), lambda i: (i, Y_CKV // C_KV_RANK)),
                  pl.BlockSpec((tm, 128), lambda i: (i, Y_SMALL // 128)),
                  pl.BlockSpec((tm, 128), lambda i: (i, 0)),
                  pl.BlockSpec((tm, 128), lambda i: (i, 0)),
                  pl.BlockSpec((1, C_KV_RANK), lambda i: (0, 0)),
                  pl.BlockSpec((C_KV_RANK, C_HEADS * C_NOPE), lambda i: (0, 0))],
        out_specs=[pl.BlockSpec((tm, C_ROW), lambda i: (i, 0)),
                   pl.BlockSpec((tm, KW), lambda i: (i, 0)),
                   pl.BlockSpec((tm, 128), lambda i: (i, 0))],
        compiler_params=_params("parallel"),
        name="mla_rows",
    )(y, y, cos_t, sin_t, g_ckv, wuk2)


ATT_TQ = 512


def _attn_finish(p_parts, v_parts, l, wuv):
    o = None
    for p, v in zip(p_parts, v_parts):
        t = jnp.dot(p.astype(BF16), v, preferred_element_type=F32)
        o = t if o is None else o + t
    o = o / l
    return jnp.dot(o.astype(BF16), wuv, preferred_element_type=F32).astype(BF16)


def _attn_real_body(q_ref, k_ref, km_ref, e_ref, em_ref, wuv_ref, o_ref):
    qi = pl.program_id(1)
    q = q_ref[...]
    km = km_ref[...]
    nt = (((1,), (1,)), ((), ()))
    s_m = lax.dot_general(q, km, nt, preferred_element_type=F32) * em_ref[...]
    for j in range(SEQ // ATT_TQ):
        @pl.when(qi == j)
        def _(j=j):
            ext = (j + 1) * ATT_TQ
            kk = k_ref[0:ext, :]
            s = lax.dot_general(q, kk, nt, preferred_element_type=F32) * e_ref[:, 0:ext]
            qpos = j * ATT_TQ + lax.broadcasted_iota(jnp.int32, (ATT_TQ, ext), 0)
            kpos = lax.broadcasted_iota(jnp.int32, (ATT_TQ, ext), 1)
            s = jnp.where(kpos <= qpos, s, -jnp.inf)
            m = jnp.maximum(jnp.max(s, axis=1, keepdims=True), jnp.max(s_m, axis=1, keepdims=True))
            p = jnp.exp(s - m)
            p_m = jnp.exp(s_m - m)
            l = jnp.sum(p, axis=1, keepdims=True) + jnp.sum(p_m, axis=1, keepdims=True)
            o_ref[...] = _attn_finish([p_m, p], [km[:, 0:C_KV_RANK], kk[:, 0:C_KV_RANK]], l, wuv_ref[...])


def _attn_real(qa, kb, et_real, et_meta, wuv):
    meta_blk = META0 // N_META
    nq = SEQ // ATT_TQ
    return pl.pallas_call(
        _attn_real_body,
        out_shape=jax.ShapeDtypeStruct((N_REAL, C_HEADS * C_V), BF16),
        grid=(BATCH, nq, C_HEADS),
        in_specs=[pl.BlockSpec((None, ATT_TQ, KW), lambda b, qi, h: (h, b * nq + qi, 0)),
                  pl.BlockSpec((SEQ, KW), lambda b, qi, h: (b, 0)),
                  pl.BlockSpec((N_META, KW), lambda b, qi, h: (meta_blk + b, 0)),
                  pl.BlockSpec((None, 1, SEQ), lambda b, qi, h: (h, 0, b)),
                  pl.BlockSpec((None, None, 1, N_META), lambda b, qi, h: (h, b, 0, 0)),
                  pl.BlockSpec((None, C_KV_RANK, C_V), lambda b, qi, h: (h, 0, 0))],
        out_specs=pl.BlockSpec((ATT_TQ, C_V), lambda b, qi, h: (b * nq + qi, h)),
        compiler_params=_params("parallel", "parallel", "parallel"),
        name="mla_attn_prompt",
    )(qa, kb, kb, et_real, et_meta, wuv)


def _attn_meta_body(q_ref, km_ref, em_ref, wuv_ref, o_ref):
    km = km_ref[...]
    nt = (((1,), (1,)), ((), ()))
    qpos = lax.broadcasted_iota(jnp.int32, (N_META, N_META), 0)
    kpos = lax.broadcasted_iota(jnp.int32, (N_META, N_META), 1)
    for h in range(C_HEADS):
        s = lax.dot_general(q_ref[h], km, nt, preferred_element_type=F32) * em_ref[h]
        s = jnp.where(kpos <= qpos, s, -jnp.inf)
        m = jnp.max(s, axis=1, keepdims=True)
        p = jnp.exp(s - m)
        l = jnp.sum(p, axis=1, keepdims=True)
        o_ref[:, h * C_V:(h + 1) * C_V] = _attn_finish([p], [km[:, 0:C_KV_RANK]], l, wuv_ref[h])


def _attn_meta(qa, kb, et_meta, wuv):
    meta_blk = META0 // N_META
    return pl.pallas_call(
        _attn_meta_body,
        out_shape=jax.ShapeDtypeStruct((BATCH * N_META, C_HEADS * C_V), BF16),
        grid=(BATCH,),
        in_specs=[pl.BlockSpec((C_HEADS, N_META, KW), lambda b: (0, meta_blk + b, 0)),
                  pl.BlockSpec((N_META, KW), lambda b: (meta_blk + b, 0)),
                  pl.BlockSpec((C_HEADS, None, 1, N_META), lambda b: (0, b, 0, 0)),
                  pl.BlockSpec((C_HEADS, C_KV_RANK, C_V), lambda b: (0, 0, 0))],
        out_specs=pl.BlockSpec((N_META, C_HEADS * C_V), lambda b: (b, 0)),
        compiler_params=_params("parallel"),
        name="mla_attn_meta",
    )(qa, kb, et_meta, wuv)


N_GROUPS = N_PAGES // PAGE_GROUP


def _paged_body(pt_ref, q_ref, kown_ref, eown_ref, cache_ref, o_ref, buf, sem, *, layer):
    b = pl.program_id(0)
    nb = pl.num_programs(0)

    def page_copy(bb, g, slot, i):
        page = pt_ref[bb, g * PAGE_GROUP + i]
        return pltpu.make_async_copy(cache_ref.at[layer, page], buf.at[slot, i], sem.at[slot])

    def start_group(bb, g, slot):
        for i in range(PAGE_GROUP):
            page_copy(bb, g, slot, i).start()

    def wait_group(bb, g, slot):
        for i in range(PAGE_GROUP):
            page_copy(bb, g, slot, i).wait()

    @pl.when(b == 0)
    def _():
        start_group(0, 0, 0)

    q = q_ref[...]
    nt = (((1,), (1,)), ((), ()))
    tn = (((0,), (0,)), ((), ()))
    kown = kown_ref[...]
    hh_r = lax.broadcasted_iota(jnp.int32, (C_HEADS, C_HEADS), 0)
    hh_c = lax.broadcasted_iota(jnp.int32, (C_HEADS, C_HEADS), 1)

    def to_col(row):
        return jnp.sum(jnp.where(hh_r == hh_c, row, 0.0), axis=1, keepdims=True)

    m0 = lax.dot_general(kown.astype(BF16), q, nt, preferred_element_type=F32) * eown_ref[...]
    l0 = jnp.ones((1, C_HEADS), F32)
    acc0 = jnp.broadcast_to(kown[:, 0:C_KV_RANK].astype(BF16).astype(F32), (C_HEADS, C_KV_RANK))

    def group(g, carry):
        m, l, acc = carry
        slot = g % 2

        @pl.when(g + 1 < N_GROUPS)
        def _():
            start_group(b, g + 1, 1 - slot)

        @pl.when(jnp.logical_and(g + 1 == N_GROUPS, b + 1 < nb))
        def _():
            start_group(b + 1, 0, 1 - slot)

        wait_group(b, g, slot)
        x = buf[slot].reshape(PAGE_GROUP * PAGE_SIZE, C_ROW)
        xb = x.astype(BF16)
        s = lax.dot_general(xb, q, nt, preferred_element_type=F32) * jnp.exp(x[:, C_KV_RANK + C_ROPE:C_ROW])
        m_new = jnp.maximum(m, jnp.max(s, axis=0, keepdims=True))
        alpha = jnp.exp(m - m_new)
        p = jnp.exp(s - m_new)
        l = alpha * l + jnp.sum(p, axis=0, keepdims=True)
        pv = lax.dot_general(p.astype(BF16), xb[:, 0:C_KV_RANK], tn, preferred_element_type=F32)
        return m_new, l, to_col(alpha) * acc + pv

    m, l, acc = lax.fori_loop(0, N_GROUPS, group, (m0, l0, acc0))
    o_ref[...] = (acc / to_col(l)).astype(BF16)


def _paged_attention(page_table, q_s, k_own, e_own, cache, layer):
    grid_spec = pltpu.PrefetchScalarGridSpec(
        num_scalar_prefetch=1,
        grid=(DEC_BATCH,),
        in_specs=[pl.BlockSpec((None, C_HEADS, C_ROW), lambda b, pt: (b, 0, 0)),
                  pl.BlockSpec((None, 1, C_ROW), lambda b, pt: (b, 0, 0)),
                  pl.BlockSpec((None, 1, C_HEADS), lambda b, pt: (b, 0, 0)),
                  pl.BlockSpec(memory_space=pl.ANY)],
        out_specs=pl.BlockSpec((None, C_HEADS, C_KV_RANK), lambda b, pt: (b, 0, 0)),
        scratch_shapes=[pltpu.VMEM((2, PAGE_GROUP, PAGE_SIZE, C_ROW), F32),
                        pltpu.SemaphoreType.DMA((2,))],
    )
    return pl.pallas_call(
        functools.partial(_paged_body, layer=layer),
        out_shape=jax.ShapeDtypeStruct((DEC_BATCH, C_HEADS, C_KV_RANK), BF16),
        grid_spec=grid_spec,
        compiler_params=_params("arbitrary"),
        name="mla_attn_paged",
    )(page_table, q_s, k_own, e_own, cache)


def _uv_body(o_ref, w_ref, out_ref):
    out_ref[...] = jnp.dot(o_ref[...], w_ref[...], preferred_element_type=F32).astype(BF16)


def _uv_sample(o_lat_t, wuv):
    return pl.pallas_call(
        _uv_body,
        out_shape=jax.ShapeDtypeStruct((DEC_BATCH, C_HEADS * C_V), BF16),
        grid=(C_HEADS,),
        in_specs=[pl.BlockSpec((None, DEC_BATCH, C_KV_RANK), lambda h: (h, 0, 0)),
                  pl.BlockSpec((None, C_KV_RANK, C_V), lambda h: (h, 0, 0))],
        out_specs=pl.BlockSpec((DEC_BATCH, C_V), lambda h: (0, h)),
        compiler_params=_params("parallel"),
        name="mla_uv_sample",
    )(o_lat_t, wuv)


FFN_TC = 256
FFN_PAD = 8


def _ffn_act(x0, x1, x2, w):
    return w[0:1, :] * x0 + w[1:2, :] * x1 + w[2:3, :] * x2


def _ffn_prompt_body(gm_ref, um_ref, g_ref, u_ref, wg_ref, wu_ref, am_ref, a_ref, gs, us):
    n = N_META + SEQ
    for src_m, src, dst in ((gm_ref, g_ref, gs), (um_ref, u_ref, us)):
        dst[0:FFN_PAD, :] = jnp.zeros((FFN_PAD, FFN_TC), F32)
        dst[FFN_PAD:FFN_PAD + N_META, :] = src_m[...]
        dst[FFN_PAD + N_META:FFN_PAD + n, :] = src[...]

    def conv(s, w_ref):
        w = w_ref[...]
        return _ffn_act(s[FFN_PAD - 2:FFN_PAD - 2 + n, :], s[FFN_PAD - 1:FFN_PAD - 1 + n, :], s[FFN_PAD:FFN_PAD + n, :], w)

    gate = conv(gs, wg_ref)
    up = conv(us, wu_ref)
    act = (gate * jax.nn.sigmoid(gate) * up).astype(BF16)
    am_ref[...] = act[0:N_META, :]
    a_ref[...] = act[N_META:n, :]


def _ffn_prompt(hup, w_conv):
    meta_blk = META0 // N_META
    nj = D_FF // FFN_TC
    return pl.pallas_call(
        _ffn_prompt_body,
        out_shape=[jax.ShapeDtypeStruct((BATCH * N_META, D_FF), BF16),
                   jax.ShapeDtypeStruct((N_REAL, D_FF), BF16)],
        grid=(BATCH, nj),
        in_specs=[pl.BlockSpec((N_META, FFN_TC), lambda b, j: (meta_blk + b, j)),
                  pl.BlockSpec((N_META, FFN_TC), lambda b, j: (meta_blk + b, nj + j)),
                  pl.BlockSpec((SEQ, FFN_TC), lambda b, j: (b, j)),
                  pl.BlockSpec((SEQ, FFN_TC), lambda b, j: (b, nj + j)),
                  pl.BlockSpec((FFN_CONV, FFN_TC), lambda b, j: (0, j)),
                  pl.BlockSpec((FFN_CONV, FFN_TC), lambda b, j: (0, nj + j))],
        out_specs=[pl.BlockSpec((N_META, FFN_TC), lambda b, j: (b, j)),
                   pl.BlockSpec((SEQ, FFN_TC), lambda b, j: (b, j))],
        scratch_shapes=[pltpu.VMEM((FFN_PAD + N_META + SEQ, FFN_TC), F32),
                        pltpu.VMEM((FFN_PAD + N_META + SEQ, FFN_TC), F32)],
        compiler_params=_params("parallel", "parallel"),
        name="ffn_act_prompt",
    )(hup, hup, hup, hup, w_conv, w_conv)


FFN_TS = 1024


def _ffn_sample_body(g_ref, u_ref, g2_ref, u2_ref, g1_ref, u1_ref, wg_ref, wu_ref, a_ref):
    gate = _ffn_act(g2_ref[...], g1_ref[...], g_ref[...], wg_ref[...])
    up = _ffn_act(u2_ref[...], u1_ref[...], u_ref[...], wu_ref[...])
    a_ref[...] = (gate * jax.nn.sigmoid(gate) * up).astype(BF16)


def _ffn_sample(hup, prev2, prev1, w_conv):
    nj = D_FF // FFN_TS
    rb = SAMP0 // DEC_BATCH

    def st(arr_is_up):
        return pl.BlockSpec((DEC_BATCH, FFN_TS), (lambda j: (0, nj + j)) if arr_is_up else (lambda j: (0, j)))

    return pl.pallas_call(
        _ffn_sample_body,
        out_shape=jax.ShapeDtypeStruct((DEC_BATCH, D_FF), BF16),
        grid=(nj,),
        in_specs=[pl.BlockSpec((DEC_BATCH, FFN_TS), lambda j: (rb, j)),
                  pl.BlockSpec((DEC_BATCH, FFN_TS), lambda j: (rb, nj + j)),
                  st(False), st(True), st(False), st(True),
                  pl.BlockSpec((FFN_CONV, FFN_TS), lambda j: (0, j)),
                  pl.BlockSpec((FFN_CONV, FFN_TS), lambda j: (0, nj + j))],
        out_specs=pl.BlockSpec((DEC_BATCH, FFN_TS), lambda j: (0, j)),
        compiler_params=_params("parallel"),
        name="ffn_act_sample",
    )(hup, hup, prev2, prev2, prev1, prev1, w_conv, w_conv)


def _join_rows(real, meta, samp):
    pad = jnp.zeros((SAMP0 - N_PROMPT, real.shape[1]), real.dtype)
    return jnp.concatenate([real, meta, pad, samp], axis=0)


def _rope_tables():
    pos = np.zeros((MP,), np.float32)
    pos[:N_REAL] = np.tile(N_META + np.arange(SEQ), BATCH)
    pos[META0:N_PROMPT] = np.tile(np.arange(N_META), BATCH)
    pos[SAMP0:] = PAST_LEN
    half = C_ROPE // 2
    inv_freq = ROPE_THETA ** (-jnp.arange(half, dtype=F32) / half)
    ang = jnp.asarray(pos)[:, None] * inv_freq[None, :]
    cos, sin = jnp.cos(ang), jnp.sin(ang)
    zeros = jnp.zeros((MP, 128 - C_ROPE), F32)
    return jnp.concatenate([cos, cos, zeros], axis=1), jnp.concatenate([-sin, sin, zeros], axis=1)


def _prompt_rows(arr, width):
    real = arr[:N_REAL].reshape(BATCH, SEQ, width)
    meta = arr[META0:N_PROMPT].reshape(BATCH, N_META, width)
    return jnp.concatenate([meta, real], axis=1)


def _layer(x, l, cos_t, sin_t, cache_mla, page_table, st_c, st_n, st_m, st_conv, st_ffn, p):
    w_in = p["w_in"][l]
    sl = lambda a, b: w_in[:, a:b]
    w_in_r = jnp.concatenate(
        [sl(1088, 5184), sl(5192, 7240), sl(7240, 19528), sl(0, 1088), sl(5184, 5192),
         jnp.zeros((D_MODEL, NY - 19528), F32)], axis=1).astype(BF16)
    xn = _rms_norm(x, p["g_mix"][l], width=D_MODEL, col_off=0, tm=352, name="norm_mix")
    y = _matmul(xn, w_in_r, tm=1056, tn=512, name="proj_in")

    bias = jnp.concatenate([p["b_i"][l], p["b_f"][l]]).astype(F32)
    gcols = y[:, Y_SMALL + LANE_AI:Y_SMALL + LANE_AI + 2 * A_HEADS]
    gt_real = gcols[:N_REAL].reshape(BATCH, SEQ // A_CHUNK, A_CHUNK, 2 * A_HEADS).transpose(0, 3, 1, 2)
    gt_meta = gcols[META0:N_PROMPT].reshape(BATCH, N_META, 2 * A_HEADS).transpose(0, 2, 1)[:, :, None, :]
    gnorm = p["g_a_norm"][l].reshape(A_HEADS, 1, A_HEAD_DIM)
    ha_m, ha_r, c_p, n_p, m_p = _mlstm_prompt(y, gt_meta, gt_real, bias, gnorm)
    ys = y[SAMP0:]
    kt = ys[:, Y_AK:Y_AK + A_WIDTH].reshape(DEC_BATCH, A_HEADS, A_HEAD_DIM).transpose(1, 2, 0)
    qt = ys[:, Y_AQ:Y_AQ + A_WIDTH].reshape(DEC_BATCH, A_HEADS, A_HEAD_DIM).transpose(1, 2, 0)
    ha_s, c_s, n_s_t, m_s_t = _mlstm_sample(
        y, kt, qt, st_c[l], st_n[l].transpose(1, 0, 2), st_m[l].T[:, :, None], bias, gnorm)
    ha = _join_rows(ha_r, ha_m, ha_s)
    mixed = _matmul(ha, p["w_a_out"][l].astype(BF16), tm=1056, tn=1024, gate=y, gate_off=Y_GA, name="proj_a")

    cb = p["b_b_conv"][l].reshape(1, B_WIDTH)
    lg = p["g_b_ln"][l].reshape(1, B_WIDTH)
    lb = p["b_b_ln"][l].reshape(1, B_WIDTH)
    hb_m, hb_r, conv_p = _conv_prompt(y, p["w_b_conv"][l], cb, lg, lb)
    hb_s, conv_s = _conv_sample(y, st_conv[l], p["w_b_conv"][l], cb, lg, lb)
    hb = _join_rows(hb_r, hb_m, hb_s)
    mixed = _matmul(hb, p["w_b_out"][l].astype(BF16), tm=1056, tn=1024, gate=y, gate_off=Y_GB, prev=mixed,
                    name="proj_b")

    w_q = p["w_q_b"][l].reshape(C_Q_RANK, C_HEADS, C_QK)
    w_q = jnp.pad(w_q, ((0, 0), (0, 0), (0, QH - C_QK))).reshape(C_Q_RANK, C_HEADS * QH).astype(BF16)
    cqn = _rms_norm(y, p["g_cq"][l], width=C_Q_RANK, col_off=Y_CQ, tm=1056, name="norm_cq")
    q = _matmul(cqn, w_q, tm=1056, tn=1024, name="proj_q")
    gain = p["g_qn"][l] * (p["g_kn"][l] * C_QK ** -0.5)
    gain_n = gain[:C_NOPE].reshape(1, C_NOPE)
    gain_r = jnp.pad(gain[C_NOPE:], (0, 128 - C_ROPE)).reshape(1, 128)
    w_uk = p["w_uk"][l]
    qa = _qpost(q, cos_t, sin_t, gain_n, gain_r, w_uk.transpose(1, 2, 0).astype(BF16), tm=352)
    rows, kb, e = _rows(y, cos_t, sin_t, p["g_ckv"][l].reshape(1, C_KV_RANK),
                        w_uk.reshape(C_KV_RANK, C_HEADS * C_NOPE).astype(BF16), tm=352)
    wuv = p["w_uv"][l].transpose(1, 0, 2).astype(BF16)
    et = e[:, :C_HEADS].T
    et_real = et[:, None, :N_REAL]
    et_meta = et[:, META0:N_PROMPT].reshape(C_HEADS, BATCH, 1, N_META)
    o_r = _attn_real(qa, kb, et_real, et_meta, wuv)
    o_m = _attn_meta(qa, kb, et_meta, wuv)
    q_s = qa[:, SAMP0:, :C_ROW].transpose(1, 0, 2)
    q_s = q_s.at[:, :, C_KV_RANK + C_ROPE:].set(0)
    o_lat_s = _paged_attention(page_table, q_s, rows[SAMP0:, None, :], e[SAMP0:, None, :C_HEADS], cache_mla, l)
    o_s = _uv_sample(o_lat_s.transpose(1, 0, 2), wuv)
    o = _join_rows(o_r, o_m, o_s)
    mixed = _matmul(o, p["w_c_out"][l].astype(BF16), tm=1056, tn=1024, gate=y, gate_off=Y_GC, prev=mixed,
                    out_dtype=BF16, name="proj_c")

    x = _matmul(mixed, p["w_out"][l].astype(BF16), tm=1056, tn=512, resid=x, name="proj_out")

    xn2 = _rms_norm(x, p["g_ffn"][l], width=D_MODEL, col_off=0, tm=352, name="norm_ffn")
    hup = _matmul(xn2, p["w_up"][l].astype(BF16), tm=1056, tn=1024, name="proj_up")
    act_m, act_r = _ffn_prompt(hup, p["w_ffn_conv"][l])
    act_s = _ffn_sample(hup, st_ffn[l][:, 0, :], st_ffn[l][:, 1, :], p["w_ffn_conv"][l])
    act = _join_rows(act_r, act_m, act_s)
    x = _matmul(act, p["w_down"][l].astype(BF16), tm=1056, tn=512, tk=4096, resid=x, name="proj_down")

    hup_s = hup[SAMP0:]
    outs = dict(
        rows_p=_prompt_rows(rows, C_ROW), rows_s=rows[SAMP0:, None, :],
        c_p=c_p, c_s=c_s,
        n_p=n_p.reshape(BATCH, A_HEADS, A_HEAD_DIM), n_s=n_s_t.transpose(1, 0, 2),
        m_p=m_p.reshape(BATCH, A_HEADS), m_s=m_s_t[:, :, 0].T,
        conv_p=conv_p, conv_s=conv_s,
        ffn_p=hup[:N_REAL].reshape(BATCH, SEQ, 2 * D_FF)[:, SEQ - (FFN_CONV - 1):],
        ffn_s=jnp.stack([st_ffn[l][:, 1, :], hup_s], axis=1),
    )
    return x, outs


def kernel(x_prompt, x_sample, cache_mla, page_table, state_mlstm_c, state_mlstm_n, state_mlstm_m, state_conv, state_ffn_conv, meta_tokens, g_mix, w_in, b_i, b_f, g_a_norm, w_a_out, w_b_conv, b_b_conv, g_b_ln, b_b_ln, w_b_out, g_cq, w_q_b, g_ckv, w_uk, w_uv, g_qn, g_kn, w_c_out, w_out, g_ffn, w_up, w_ffn_conv, w_down):
    p = dict(g_mix=g_mix, w_in=w_in, b_i=b_i, b_f=b_f, g_a_norm=g_a_norm, w_a_out=w_a_out, w_b_conv=w_b_conv,
             b_b_conv=b_b_conv, g_b_ln=g_b_ln, b_b_ln=b_b_ln, w_b_out=w_b_out, g_cq=g_cq, w_q_b=w_q_b,
             g_ckv=g_ckv, w_uk=w_uk, w_uv=w_uv, g_qn=g_qn, g_kn=g_kn, w_c_out=w_c_out, w_out=w_out,
             g_ffn=g_ffn, w_up=w_up, w_ffn_conv=w_ffn_conv, w_down=w_down)
    x = _join_rows(x_prompt.reshape(N_REAL, D_MODEL),
                   jnp.tile(meta_tokens.astype(x_prompt.dtype), (BATCH, 1)),
                   x_sample.reshape(DEC_BATCH, D_MODEL))
    cos_t, sin_t = _rope_tables()
    per_layer = []
    for l in range(DEPTH):
        x, outs = _layer(x, l, cos_t, sin_t, cache_mla, page_table, state_mlstm_c, state_mlstm_n, state_mlstm_m,
                         state_conv, state_ffn_conv, p)
        per_layer.append(outs)
    stack = lambda name: jnp.stack([o[name] for o in per_layer])
    return (x[:N_REAL].reshape(BATCH, SEQ, D_MODEL), x[SAMP0:].reshape(DEC_BATCH, 1, D_MODEL),
            stack("rows_p"), stack("rows_s"), stack("c_p"), stack("c_s"), stack("n_p"), stack("n_s"),
            stack("m_p"), stack("m_s"), stack("conv_p"), stack("conv_s"), stack("ffn_p"), stack("ffn_s"))
```

```python
import functools

import jax
import jax.numpy as jnp
import numpy as np
from jax import lax
from jax.experimental import pallas as pl
from jax.experimental.pallas import tpu as pltpu

F32 = jnp.float32
BF16 = jnp.bfloat16

D_MODEL = 4096
BATCH = 4
SEQ = 2048
DEPTH = 2
DEC_BATCH = 128
PAST_LEN = 16384
PAGE_SIZE = 128
N_PAGES = PAST_LEN // PAGE_SIZE
N_META = 16
A_HEADS = 4
A_HEAD_DIM = 256
A_WIDTH = A_HEADS * A_HEAD_DIM
A_CHUNK = 64
B_WIDTH = 1024
B_CONV = 31
C_HEADS = 16
C_NOPE = 128
C_ROPE = 64
C_QK = C_NOPE + C_ROPE
C_V = 128
C_Q_RANK = 768
C_KV_RANK = 256
C_ROW = C_KV_RANK + C_ROPE + C_HEADS
ROPE_THETA = 10000.0
D_FF = 8192
FFN_CONV = 3
EPS = 1e-6

N_REAL = BATCH * SEQ
META0 = N_REAL
N_PROMPT = N_REAL + BATCH * N_META
SAMP0 = 8320
MP = SAMP0 + DEC_BATCH

Y_AQ, Y_AK, Y_AV, Y_AO = 0, 1024, 2048, 3072
Y_BU, Y_BG = 4096, 5120
Y_GA, Y_GB, Y_GC = 6144, 10240, 14336
Y_CQ, Y_CKV, Y_SMALL = 18432, 19200, 19456
NY = 19968
LANE_AI, LANE_AF = 64, 68

QH = 256
KW = 384

VMEM_LIMIT = 56 * 1024 * 1024
PAGE_GROUP = 8
PAGED_SEQS = 2


def _params(*sem):
    return pltpu.CompilerParams(dimension_semantics=sem, vmem_limit_bytes=VMEM_LIMIT)


def _mm_body(*refs, nk, mode, has_prev):
    x_ref, w_ref = refs[0], refs[1]
    idx = 2
    r_ref = g_ref = p_ref = None
    if mode == "resid":
        r_ref = refs[idx]; idx += 1
    if mode == "gate":
        g_ref = refs[idx]; idx += 1
        if has_prev:
            p_ref = refs[idx]; idx += 1
    o_ref = refs[idx]; idx += 1
    acc_ref = refs[idx] if nk > 1 else None

    def epilogue(acc):
        if mode == "resid":
            acc = r_ref[...] + acc
        elif mode == "gate":
            acc = jax.nn.sigmoid(g_ref[...]) * acc
            if p_ref is not None:
                acc = p_ref[...] + acc
        o_ref[...] = acc.astype(o_ref.dtype)

    part = jnp.dot(x_ref[...], w_ref[...], preferred_element_type=F32)
    if nk == 1:
        epilogue(part)
    else:
        k = pl.program_id(2)

        @pl.when(k == 0)
        def _():
            acc_ref[...] = part

        @pl.when(jnp.logical_and(k > 0, k < nk - 1))
        def _():
            acc_ref[...] += part

        @pl.when(k == nk - 1)
        def _():
            epilogue(acc_ref[...] + part)


def _matmul(x, w, *, tm, tn, tk=None, out_dtype=F32, resid=None, gate=None, gate_off=0, prev=None, layer=None, name):
    m, kdim = x.shape
    n = w.shape[-1]
    tk = kdim if tk is None else tk
    nk = kdim // tk
    assert m % tm == 0 and n % tn == 0 and kdim % tk == 0 and gate_off % tn == 0
    mode = "resid" if resid is not None else ("gate" if gate is not None else "plain")
    if w.ndim == 3:
        w_spec = pl.BlockSpec((None, tk, tn), lambda i, j, k: (layer, k, j))
    else:
        w_spec = pl.BlockSpec((tk, tn), lambda i, j, k: (k, j))
    in_specs = [pl.BlockSpec((tm, tk), lambda i, j, k: (i, k)), w_spec]
    args = [x, w]
    if resid is not None:
        in_specs.append(pl.BlockSpec((tm, tn), lambda i, j, k: (i, j)))
        args.append(resid)
    if gate is not None:
        goff = gate_off // tn
        in_specs.append(pl.BlockSpec((tm, tn), lambda i, j, k: (i, goff + j)))
        args.append(gate)
        if prev is not None:
            in_specs.append(pl.BlockSpec((tm, tn), lambda i, j, k: (i, j)))
            args.append(prev)
    return pl.pallas_call(
        functools.partial(_mm_body, nk=nk, mode=mode, has_prev=prev is not None),
        out_shape=jax.ShapeDtypeStruct((m, n), out_dtype),
        grid=(m // tm, n // tn, nk),
        in_specs=in_specs,
        out_specs=pl.BlockSpec((tm, tn), lambda i, j, k: (i, j)),
        scratch_shapes=[pltpu.VMEM((tm, tn), F32)] if nk > 1 else [],
        compiler_params=_params("parallel", "parallel", "arbitrary"),
        name=name,
    )(*args)


def _rms_body(x_ref, g_ref, o_ref):
    x = x_ref[...]
    ms = jnp.mean(x * x, axis=-1, keepdims=True)
    o_ref[...] = (x * lax.rsqrt(ms + EPS) * g_ref[...]).astype(o_ref.dtype)


def _rms_norm(x, g, *, width, col_off, tm, name):
    m = x.shape[0]
    cblk = col_off // width
    return pl.pallas_call(
        _rms_body,
        out_shape=jax.ShapeDtypeStruct((m, width), BF16),
        grid=(m // tm,),
        in_specs=[pl.BlockSpec((tm, width), lambda i: (i, cblk)),
                  pl.BlockSpec((1, width), lambda i: (0, 0))],
        out_specs=pl.BlockSpec((tm, width), lambda i: (i, 0)),
        compiler_params=_params("parallel"),
        name=name,
    )(x, g.reshape(1, width))


def _lane_pick(g, lane_idx):
    lane = lax.broadcasted_iota(jnp.int32, g.shape, 1)
    return jnp.sum(jnp.where(lane == lane_idx, g, 0.0), axis=1, keepdims=True)


def _head_norm_gate(hh, gn, ao):
    ms = jnp.mean(hh * hh, axis=-1, keepdims=True)
    return (hh * lax.rsqrt(ms + EPS) * gn * jax.nn.sigmoid(ao)).astype(BF16)


def _mlstm_prompt_body(bias_ref, qm_ref, km_ref, vm_ref, aom_ref, gm_ref, irm_ref, frm_ref,
                       q_ref, k_ref, v_ref, ao_ref, g_ref, irr_ref, frr_ref, gn_ref,
                       hm_ref, h_ref, c_ref, n_ref, m_ref):
    h = pl.program_id(1)
    b_i = bias_ref[h]
    b_f = bias_ref[A_HEADS + h]
    gn = gn_ref[...]

    def chunk(length, q, k, v, ao, g, ai_row, af_row, n_prev, m_prev):
        k = k * (A_HEAD_DIM ** -0.5)
        it_col = _lane_pick(g, LANE_AI + h) + b_i
        lf_col = jax.nn.log_sigmoid(_lane_pick(g, LANE_AF + h) + b_f)
        it_row = ai_row + b_i
        lf_row = jax.nn.log_sigmoid(af_row + b_f)
        t_i = lax.broadcasted_iota(jnp.int32, (length, length), 0)
        s_i = lax.broadcasted_iota(jnp.int32, (length, length), 1)
        causal = s_i <= t_i
        b_col = jnp.sum(jnp.where(causal, lf_row, 0.0), axis=1, keepdims=True)
        b_row = jnp.sum(jnp.where(t_i <= s_i, lf_col, 0.0), axis=0, keepdims=True)
        log_d = jnp.where(causal, b_col - b_row + it_row, -jnp.inf)
        log_inter = b_col + m_prev
        m_t = jnp.maximum(log_inter, jnp.max(log_d, axis=1, keepdims=True))
        d = jnp.exp(log_d - m_t)
        w_inter = jnp.exp(log_inter - m_t)
        qb = q.astype(BF16)
        kb = k.astype(BF16)
        vb = v.astype(BF16)
        c_prev = c_ref[...]
        s = lax.dot_general(qb, kb, (((1,), (1,)), ((), ())), preferred_element_type=F32) * d
        num = (w_inter * jnp.dot(qb, c_prev.astype(BF16), preferred_element_type=F32)
               + jnp.dot(s.astype(BF16), vb, preferred_element_type=F32))
        den = w_inter * jnp.sum(q * n_prev, axis=1, keepdims=True) + jnp.sum(s, axis=1, keepdims=True)
        hh = num / jnp.maximum(jnp.abs(den), jnp.exp(-m_t))
        m_new = m_t[length - 1:length, :]
        b_last = b_col[length - 1:length, :]
        w_end = jnp.exp(b_last - b_col + it_col - m_new)
        decay = jnp.exp(b_last + m_prev - m_new)
        kw = k * w_end
        c_ref[...] = decay * c_prev + lax.dot_general(
            kw.astype(BF16), vb, (((0,), (0,)), ((), ())), preferred_element_type=F32)
        n_new = decay * n_prev + jnp.sum(kw, axis=0, keepdims=True)
        return _head_norm_gate(hh, gn, ao), n_new, m_new

    c_ref[...] = jnp.zeros_like(c_ref)
    n0 = jnp.zeros((1, A_HEAD_DIM), F32)
    m0 = jnp.zeros((1, 1), F32)
    out, n1, m1 = chunk(N_META, qm_ref[...], km_ref[...], vm_ref[...], aom_ref[...], gm_ref[...],
                        irm_ref[...], frm_ref[...], n0, m0)
    hm_ref[...] = out

    def step(i, carry):
        n_prev, m_prev = carry
        r0 = pl.multiple_of(i * A_CHUNK, A_CHUNK)
        rows = pl.ds(r0, A_CHUNK)
        out, n_new, m_new = chunk(A_CHUNK, q_ref[rows, :], k_ref[rows, :], v_ref[rows, :], ao_ref[rows, :],
                                  g_ref[rows, :], irr_ref[pl.ds(i, 1), :], frr_ref[pl.ds(i, 1), :],
                                  n_prev, m_prev)
        h_ref[rows, :] = out
        return n_new, m_new

    n_fin, m_fin = lax.fori_loop(0, SEQ // A_CHUNK, step, (n1, m1))
    n_ref[...] = n_fin
    m_ref[...] = m_fin


def _mlstm_prompt(y, gt_meta, gt_real, bias, gnorm):
    meta_blk = META0 // N_META

    def ycol(off):
        return off // A_HEAD_DIM

    def meta_spec(off):
        return pl.BlockSpec((N_META, A_HEAD_DIM), lambda b, h: (meta_blk + b, ycol(off) + h))

    def real_spec(off):
        return pl.BlockSpec((SEQ, A_HEAD_DIM), lambda b, h: (b, ycol(off) + h))

    in_specs = [
        pl.BlockSpec(memory_space=pltpu.SMEM),
        meta_spec(Y_AQ), meta_spec(Y_AK), meta_spec(Y_AV), meta_spec(Y_AO),
        pl.BlockSpec((N_META, 128), lambda b, h: (meta_blk + b, Y_SMALL // 128)),
        pl.BlockSpec((None, None, 1, N_META), lambda b, h: (b, h, 0, 0)),
        pl.BlockSpec((None, None, 1, N_META), lambda b, h: (b, A_HEADS + h, 0, 0)),
        real_spec(Y_AQ), real_spec(Y_AK), real_spec(Y_AV), real_spec(Y_AO),
        pl.BlockSpec((SEQ, 128), lambda b, h: (b, Y_SMALL // 128)),
        pl.BlockSpec((None, None, SEQ // A_CHUNK, A_CHUNK), lambda b, h: (b, h, 0, 0)),
        pl.BlockSpec((None, None, SEQ // A_CHUNK, A_CHUNK), lambda b, h: (b, A_HEADS + h, 0, 0)),
        pl.BlockSpec((None, 1, A_HEAD_DIM), lambda b, h: (h, 0, 0)),
    ]
    out_shape = [
        jax.ShapeDtypeStruct((BATCH * N_META, A_WIDTH), BF16),
        jax.ShapeDtypeStruct((N_REAL, A_WIDTH), BF16),
        jax.ShapeDtypeStruct((BATCH, A_HEADS, A_HEAD_DIM, A_HEAD_DIM), F32),
        jax.ShapeDtypeStruct((BATCH, A_HEADS, 1, A_HEAD_DIM), F32),
        jax.ShapeDtypeStruct((BATCH, A_HEADS, 1, 1), F32),
    ]
    out_specs = [
        pl.BlockSpec((N_META, A_HEAD_DIM), lambda b, h: (b, h)),
        pl.BlockSpec((SEQ, A_HEAD_DIM), lambda b, h: (b, h)),
        pl.BlockSpec((None, None, A_HEAD_DIM, A_HEAD_DIM), lambda b, h: (b, h, 0, 0)),
        pl.BlockSpec((None, None, 1, A_HEAD_DIM), lambda b, h: (b, h, 0, 0)),
        pl.BlockSpec((None, None, 1, 1), lambda b, h: (b, h, 0, 0)),
    ]
    return pl.pallas_call(
        _mlstm_prompt_body,
        out_shape=out_shape,
        grid=(BATCH, A_HEADS),
        in_specs=in_specs,
        out_specs=out_specs,
        compiler_params=_params("parallel", "parallel"),
        name="mlstm_prompt",
    )(bias, y, y, y, y, y, gt_meta, gt_meta, y, y, y, y, y, gt_real, gt_real, gnorm)


MS_NB = 8


def _mlstm_sample_body(bias_ref, q_ref, k_ref, v_ref, ao_ref, g_ref, kt_ref, qt_ref, c_ref, n_ref, m_ref, gn_ref,
                       h_ref, co_ref, no_ref, mo_ref):
    h = pl.program_id(0)
    bt = pl.program_id(1)
    b_i = bias_ref[h]
    b_f = bias_ref[A_HEADS + h]
    q = q_ref[...]
    k = k_ref[...] * (A_HEAD_DIM ** -0.5)
    v = v_ref[...]
    g = g_ref[...]
    it = _lane_pick(g, LANE_AI + h) + b_i
    lf = jax.nn.log_sigmoid(_lane_pick(g, LANE_AF + h) + b_f)
    m_prev = m_ref[...]
    n_prev = n_ref[...]
    log_inter = lf + m_prev
    m_t = jnp.maximum(log_inter, it)
    d = jnp.exp(it - m_t)
    w_inter = jnp.exp(log_inter - m_t)
    s = jnp.sum(q * k, axis=1, keepdims=True) * d
    den = w_inter * jnp.sum(q * n_prev, axis=1, keepdims=True) + s
    wv = d * v
    lane = lax.broadcasted_iota(jnp.int32, (A_HEAD_DIM, DEC_BATCH), 1)
    kt = kt_ref[...] * (A_HEAD_DIM ** -0.5)
    qt = qt_ref[...]
    qc_rows = []
    for j in range(MS_NB):
        sel = lane == bt * MS_NB + j
        k_col = jnp.sum(jnp.where(sel, kt, 0.0), axis=1, keepdims=True)
        q_col = jnp.sum(jnp.where(sel, qt, 0.0), axis=1, keepdims=True)
        c_prev = c_ref[j]
        qc_rows.append(jnp.sum(q_col * c_prev, axis=0, keepdims=True))
        co_ref[j] = w_inter[j:j + 1, :] * c_prev + k_col * wv[j:j + 1, :]
    qc = jnp.concatenate(qc_rows, axis=0)
    num = w_inter * qc + s * v
    hh = num / jnp.maximum(jnp.abs(den), jnp.exp(-m_t))
    h_ref[...] = _head_norm_gate(hh, gn_ref[...], ao_ref[...])
    no_ref[...] = w_inter * n_prev + d * k
    mo_ref[...] = m_t


def _mlstm_sample(y, kt, qt, c_state, n_state_t, m_state_t, bias, gnorm, layer):
    rb0 = SAMP0 // MS_NB

    def yspec(off):
        return pl.BlockSpec((MS_NB, A_HEAD_DIM), lambda h, bt: (rb0 + bt, off // A_HEAD_DIM + h))

    in_specs = [
        pl.BlockSpec(memory_space=pltpu.SMEM),
        yspec(Y_AQ), yspec(Y_AK), yspec(Y_AV), yspec(Y_AO),
        pl.BlockSpec((MS_NB, 128), lambda h, bt: (rb0 + bt, Y_SMALL // 128)),
        pl.BlockSpec((None, A_HEAD_DIM, DEC_BATCH), lambda h, bt: (h, 0, 0)),
        pl.BlockSpec((None, A_HEAD_DIM, DEC_BATCH), lambda h, bt: (h, 0, 0)),
        pl.BlockSpec((None, MS_NB, None, A_HEAD_DIM, A_HEAD_DIM), lambda h, bt: (layer, bt, h, 0, 0)),
        pl.BlockSpec((None, MS_NB, A_HEAD_DIM), lambda h, bt: (h, bt, 0)),
        pl.BlockSpec((None, MS_NB, 1), lambda h, bt: (h, bt, 0)),
        pl.BlockSpec((None, 1, A_HEAD_DIM), lambda h, bt: (h, 0, 0)),
    ]
    out_shape = [
        jax.ShapeDtypeStruct((DEC_BATCH, A_WIDTH), BF16),
        jax.ShapeDtypeStruct((DEC_BATCH, A_HEADS, A_HEAD_DIM, A_HEAD_DIM), F32),
        jax.ShapeDtypeStruct((A_HEADS, DEC_BATCH, A_HEAD_DIM), F32),
        jax.ShapeDtypeStruct((A_HEADS, DEC_BATCH, 1), F32),
    ]
    out_specs = [
        pl.BlockSpec((MS_NB, A_HEAD_DIM), lambda h, bt: (bt, h)),
        pl.BlockSpec((MS_NB, None, A_HEAD_DIM, A_HEAD_DIM), lambda h, bt: (bt, h, 0, 0)),
        pl.BlockSpec((None, MS_NB, A_HEAD_DIM), lambda h, bt: (h, bt, 0)),
        pl.BlockSpec((None, MS_NB, 1), lambda h, bt: (h, bt, 0)),
    ]
    return pl.pallas_call(
        _mlstm_sample_body,
        out_shape=out_shape,
        grid=(A_HEADS, DEC_BATCH // MS_NB),
        in_specs=in_specs,
        out_specs=out_specs,
        compiler_params=_params("parallel", "parallel"),
        name="mlstm_sample",
    )(bias, y, y, y, y, y, kt, qt, c_state, n_state_t, m_state_t, gnorm)


CONV_PAD = 32
CONV_ROWS = 32


def _ln_swish(u, bias, g, b):
    u = u + bias
    mu = jnp.mean(u, axis=-1, keepdims=True)
    var = jnp.mean(jnp.square(u - mu), axis=-1, keepdims=True)
    y = (u - mu) * lax.rsqrt(var + EPS) * g + b
    return (y * jax.nn.sigmoid(y)).astype(BF16)


def _conv_prompt_body(bum_ref, bgm_ref, bu_ref, bg_ref, w_ref, cb_ref, lg_ref, lb_ref,
                      hm_ref, h_ref, st_ref, u_s):
    seq0 = CONV_PAD + N_META
    u_s[0:CONV_PAD, :] = jnp.zeros((CONV_PAD, B_WIDTH), F32)
    u_s[CONV_PAD:seq0, :] = bum_ref[...] * jax.nn.sigmoid(bgm_ref[...])

    def glu(i, _):
        rows = pl.ds(pl.multiple_of(i * 256, 256), 256)
        u_s[pl.ds(pl.multiple_of(seq0 + i * 256, 16), 256), :] = bu_ref[rows, :] * jax.nn.sigmoid(bg_ref[rows, :])
        return 0

    lax.fori_loop(0, SEQ // 256, glu, 0)
    bias, lg, lb = cb_ref[...], lg_ref[...], lb_ref[...]

    def conv_rows(start, nrows):
        win = u_s[pl.ds(start, nrows + CONV_PAD), :]
        acc = jnp.zeros((nrows, B_WIDTH), F32)
        for b in range(8):
            taps = range(b, B_CONV, 8)
            shifted = win[2 + b:2 + b + nrows + taps[-1] - b, :]
            for k in taps:
                acc = acc + w_ref[k:k + 1, :] * shifted[k - b:k - b + nrows, :]
        return _ln_swish(acc, bias, lg, lb)

    hm_ref[...] = conv_rows(0, N_META)

    def step(i, _):
        r0 = pl.multiple_of(i * CONV_ROWS, CONV_ROWS)
        h_ref[pl.ds(r0, CONV_ROWS), :] = conv_rows(pl.multiple_of(N_META + r0, 8), CONV_ROWS)
        return 0

    lax.fori_loop(0, SEQ // CONV_ROWS, step, 0)
    st_ref[...] = u_s[CONV_PAD + N_META + SEQ - (B_CONV - 1):CONV_PAD + N_META + SEQ, :]


def _conv_prompt(y, w, cb, lg, lb):
    meta_blk = META0 // N_META
    cu, cg = Y_BU // B_WIDTH, Y_BG // B_WIDTH
    vec = pl.BlockSpec((1, B_WIDTH), lambda b: (0, 0))
    return pl.pallas_call(
        _conv_prompt_body,
        out_shape=[jax.ShapeDtypeStruct((BATCH * N_META, B_WIDTH), BF16),
                   jax.ShapeDtypeStruct((N_REAL, B_WIDTH), BF16),
                   jax.ShapeDtypeStruct((BATCH, B_CONV - 1, B_WIDTH), F32)],
        grid=(BATCH,),
        in_specs=[pl.BlockSpec((N_META, B_WIDTH), lambda b: (meta_blk + b, cu)),
                  pl.BlockSpec((N_META, B_WIDTH), lambda b: (meta_blk + b, cg)),
                  pl.BlockSpec((SEQ, B_WIDTH), lambda b: (b, cu)),
                  pl.BlockSpec((SEQ, B_WIDTH), lambda b: (b, cg)),
                  pl.BlockSpec((B_CONV, B_WIDTH), lambda b: (0, 0)),
                  vec, vec, vec],
        out_specs=[pl.BlockSpec((N_META, B_WIDTH), lambda b: (b, 0)),
                   pl.BlockSpec((SEQ, B_WIDTH), lambda b: (b, 0)),
                   pl.BlockSpec((None, B_CONV - 1, B_WIDTH), lambda b: (b, 0, 0))],
        scratch_shapes=[pltpu.VMEM((CONV_PAD + N_META + SEQ, B_WIDTH), F32)],
        compiler_params=_params("parallel"),
        name="conv_prompt",
    )(y, y, y, y, w, cb, lg, lb)


CS_NB = 8


def _conv_sample_body(bu_ref, bg_ref, st_ref, w_ref, cb_ref, lg_ref, lb_ref, h_ref, so_ref):
    u = bu_ref[...] * jax.nn.sigmoid(bg_ref[...])
    w_hist = w_ref[0:B_CONV - 1, :]
    hist_rows = []
    for j in range(CS_NB):
        st = st_ref[j]
        hist_rows.append(jnp.sum(st * w_hist, axis=0, keepdims=True))
        so_ref[j, 0:B_CONV - 2, :] = st[1:B_CONV - 1, :]
        so_ref[j, B_CONV - 2:B_CONV - 1, :] = u[j:j + 1, :]
    acc = jnp.concatenate(hist_rows, axis=0) + w_ref[B_CONV - 1:B_CONV, :] * u
    h_ref[...] = _ln_swish(acc, cb_ref[...], lg_ref[...], lb_ref[...])


def _conv_sample(y, state, w, cb, lg, lb):
    rb0 = SAMP0 // CS_NB
    vec = pl.BlockSpec((1, B_WIDTH), lambda i: (0, 0))
    return pl.pallas_call(
        _conv_sample_body,
        out_shape=[jax.ShapeDtypeStruct((DEC_BATCH, B_WIDTH), BF16),
                   jax.ShapeDtypeStruct((DEC_BATCH, B_CONV - 1, B_WIDTH), F32)],
        grid=(DEC_BATCH // CS_NB,),
        in_specs=[pl.BlockSpec((CS_NB, B_WIDTH), lambda i: (rb0 + i, Y_BU // B_WIDTH)),
                  pl.BlockSpec((CS_NB, B_WIDTH), lambda i: (rb0 + i, Y_BG // B_WIDTH)),
                  pl.BlockSpec((CS_NB, B_CONV - 1, B_WIDTH), lambda i: (i, 0, 0)),
                  pl.BlockSpec((B_CONV, B_WIDTH), lambda i: (0, 0)),
                  vec, vec, vec],
        out_specs=[pl.BlockSpec((CS_NB, B_WIDTH), lambda i: (i, 0)),
                   pl.BlockSpec((CS_NB, B_CONV - 1, B_WIDTH), lambda i: (i, 0, 0))],
        compiler_params=_params("parallel"),
        name="conv_sample",
    )(y, y, state, w, cb, lg, lb)


def _rope_rotate(x, cos_t, sin_t):
    lane = lax.broadcasted_iota(jnp.int32, x.shape, 1)
    swapped = jnp.where(lane < C_ROPE // 2, pltpu.roll(x, 128 - C_ROPE // 2, 1), pltpu.roll(x, C_ROPE // 2, 1))
    return x * cos_t + swapped * sin_t


def _qpost_body(q_ref, cos_ref, sin_ref, gn_ref, gr_ref, wuk_ref, o_ref):
    cos_t, sin_t = cos_ref[...], sin_ref[...]
    gn, gr = gn_ref[...], gr_ref[...]
    for h in range(C_HEADS):
        nope = q_ref[:, h * QH:h * QH + C_NOPE]
        rot = _rope_rotate(q_ref[:, h * QH + C_NOPE:(h + 1) * QH], cos_t, sin_t)
        ms = (jnp.sum(nope * nope, axis=1, keepdims=True) + jnp.sum(rot * rot, axis=1, keepdims=True)) / C_QK
        r = lax.rsqrt(ms + EPS)
        q_lat = jnp.dot((nope * r * gn).astype(BF16), wuk_ref[h], preferred_element_type=F32)
        o_ref[h, :, 0:C_KV_RANK] = q_lat.astype(BF16)
        o_ref[h, :, C_KV_RANK:KW] = (rot * r * gr).astype(BF16)


def _qpost(q, cos_t, sin_t, gain_n, gain_r, wuk_t, *, tm):
    return pl.pallas_call(
        _qpost_body,
        out_shape=jax.ShapeDtypeStruct((C_HEADS, MP, KW), BF16),
        grid=(MP // tm,),
        in_specs=[pl.BlockSpec((tm, C_HEADS * QH), lambda i: (i, 0)),
                  pl.BlockSpec((tm, 128), lambda i: (i, 0)),
                  pl.BlockSpec((tm, 128), lambda i: (i, 0)),
                  pl.BlockSpec((1, 128), lambda i: (0, 0)),
                  pl.BlockSpec((1, 128), lambda i: (0, 0)),
                  pl.BlockSpec((C_HEADS, C_NOPE, C_KV_RANK), lambda i: (0, 0, 0))],
        out_specs=pl.BlockSpec((C_HEADS, tm, KW), lambda i: (0, i, 0)),
        compiler_params=_params("parallel"),
        name="mla_qpost",
    )(q, cos_t, sin_t, gain_n, gain_r, wuk_t)


def _rows_body(ckv_ref, sm_ref, cos_ref, sin_ref, g_ref, wuk_ref, rows_ref, kb_ref, e_ref):
    x = ckv_ref[...]
    c = x * lax.rsqrt(jnp.mean(x * x, axis=-1, keepdims=True) + EPS) * g_ref[...]
    k_r = _rope_rotate(sm_ref[...], cos_ref[...], sin_ref[...])
    k_nope = jnp.dot(c.astype(BF16), wuk_ref[...], preferred_element_type=F32)
    kr_ss = jnp.sum(k_r * k_r, axis=1, keepdims=True)
    lane = lax.broadcasted_iota(jnp.int32, k_r.shape, 1)
    kls = jnp.zeros(k_r.shape, F32)
    for h in range(C_HEADS):
        kn = k_nope[:, h * C_NOPE:(h + 1) * C_NOPE]
        ms = (jnp.sum(kn * kn, axis=1, keepdims=True) + kr_ss) / C_QK
        kls = jnp.where(lane == h, -0.5 * jnp.log(ms + EPS), kls)
    rows_ref[:, 0:C_KV_RANK] = c
    tail = k_r + pltpu.roll(kls, C_ROPE, 1)
    rows_ref[:, C_KV_RANK:C_ROW] = tail[:, 0:C_ROW - C_KV_RANK]
    kb_ref[:, 0:C_KV_RANK] = c.astype(BF16)
    kb_ref[:, C_KV_RANK:KW] = k_r.astype(BF16)
    e_ref[...] = jnp.where(lane < C_HEADS, jnp.exp(kls), 0.0)


def _rows(y, cos_t, sin_t, g_ckv, wuk2, *, tm):
    return pl.pallas_call(
        _rows_body,
        out_shape=[jax.ShapeDtypeStruct((MP, C_ROW), F32),
                   jax.ShapeDtypeStruct((MP, KW), BF16),
                   jax.ShapeDtypeStruct((MP, 128), F32)],
        grid=(MP // tm,),
        in_specs=[pl.BlockSpec((tm, C_KV_RANK), lambda i: (i, Y_CKV // C_KV_RANK)),
                  pl.BlockSpec((tm, 128), lambda i: (i, Y_SMALL // 128)),
                  pl.BlockSpec((tm, 128), lambda i: (i, 0)),
                  pl.BlockSpec((tm, 128), lambda i: (i, 0)),
                  pl.BlockSpec((1, C_KV_RANK), lambda i: (0, 0)),
                  pl.BlockSpec((C_KV_RANK, C_HEADS * C_NOPE), lambda i: (0, 0))],
        out_specs=[pl.BlockSpec((tm, C_ROW), lambda i: (i, 0)),
                   pl.BlockSpec((tm, KW), lambda i: (i, 0)),
                   pl.BlockSpec((tm, 128), lambda i: (i, 0))],
        compiler_params=_params("parallel"),
        name="mla_rows",
    )(y, y, cos_t, sin_t, g_ckv, wuk2)


ATT_TQ = 512


def _attn_finish(p_parts, v_parts, l, wuv):
    o = None
    for p, v in zip(p_parts, v_parts):
        t = jnp.dot(p.astype(BF16), v, preferred_element_type=F32)
        o = t if o is None else o + t
    o = o / l
    return jnp.dot(o.astype(BF16), wuv, preferred_element_type=F32).astype(BF16)


def _attn_real_body(q_ref, k_ref, km_ref, e_ref, em_ref, wuv_ref, o_ref):
    qi = pl.program_id(1)
    q = q_ref[...]
    km = km_ref[...]
    nt = (((1,), (1,)), ((), ()))
    s_m = lax.dot_general(q, km, nt, preferred_element_type=F32) * em_ref[...]
    for j in range(SEQ // ATT_TQ):
        @pl.when(qi == j)
        def _(j=j):
            ext = (j + 1) * ATT_TQ
            kk = k_ref[0:ext, :]
            s = lax.dot_general(q, kk, nt, preferred_element_type=F32) * e_ref[:, 0:ext]
            qpos = j * ATT_TQ + lax.broadcasted_iota(jnp.int32, (ATT_TQ, ext), 0)
            kpos = lax.broadcasted_iota(jnp.int32, (ATT_TQ, ext), 1)
            s = jnp.where(kpos <= qpos, s, -jnp.inf)
            m = jnp.maximum(jnp.max(s, axis=1, keepdims=True), jnp.max(s_m, axis=1, keepdims=True))
            p = jnp.exp(s - m)
            p_m = jnp.exp(s_m - m)
            l = jnp.sum(p, axis=1, keepdims=True) + jnp.sum(p_m, axis=1, keepdims=True)
            o_ref[...] = _attn_finish([p_m, p], [km[:, 0:C_KV_RANK], kk[:, 0:C_KV_RANK]], l, wuv_ref[...])


def _attn_real(qa, kb, et_real, et_meta, wuv):
    meta_blk = META0 // N_META
    nq = SEQ // ATT_TQ
    return pl.pallas_call(
        _attn_real_body,
        out_shape=jax.ShapeDtypeStruct((N_REAL, C_HEADS * C_V), BF16),
        grid=(BATCH, nq, C_HEADS),
        in_specs=[pl.BlockSpec((None, ATT_TQ, KW), lambda b, qi, h: (h, b * nq + qi, 0)),
                  pl.BlockSpec((SEQ, KW), lambda b, qi, h: (b, 0)),
                  pl.BlockSpec((N_META, KW), lambda b, qi, h: (meta_blk + b, 0)),
                  pl.BlockSpec((None, 1, SEQ), lambda b, qi, h: (h, 0, b)),
                  pl.BlockSpec((None, None, 1, N_META), lambda b, qi, h: (h, b, 0, 0)),
                  pl.BlockSpec((None, C_KV_RANK, C_V), lambda b, qi, h: (h, 0, 0))],
        out_specs=pl.BlockSpec((ATT_TQ, C_V), lambda b, qi, h: (b * nq + qi, h)),
        compiler_params=_params("parallel", "parallel", "parallel"),
        name="mla_attn_prompt",
    )(qa, kb, kb, et_real, et_meta, wuv)


def _attn_meta_body(q_ref, km_ref, em_ref, wuv_ref, o_ref):
    km = km_ref[...]
    nt = (((1,), (1,)), ((), ()))
    qpos = lax.broadcasted_iota(jnp.int32, (N_META, N_META), 0)
    kpos = lax.broadcasted_iota(jnp.int32, (N_META, N_META), 1)
    for h in range(C_HEADS):
        s = lax.dot_general(q_ref[h], km, nt, preferred_element_type=F32) * em_ref[h]
        s = jnp.where(kpos <= qpos, s, -jnp.inf)
        m = jnp.max(s, axis=1, keepdims=True)
        p = jnp.exp(s - m)
        l = jnp.sum(p, axis=1, keepdims=True)
        o_ref[:, h * C_V:(h + 1) * C_V] = _attn_finish([p], [km[:, 0:C_KV_RANK]], l, wuv_ref[h])


def _attn_meta(qa, kb, et_meta, wuv):
    meta_blk = META0 // N_META
    return pl.pallas_call(
        _attn_meta_body,
        out_shape=jax.ShapeDtypeStruct((BATCH * N_META, C_HEADS * C_V), BF16),
        grid=(BATCH,),
        in_specs=[pl.BlockSpec((C_HEADS, N_META, KW), lambda b: (0, meta_blk + b, 0)),
                  pl.BlockSpec((N_META, KW), lambda b: (meta_blk + b, 0)),
                  pl.BlockSpec((C_HEADS, None, 1, N_META), lambda b: (0, b, 0, 0)),
                  pl.BlockSpec((C_HEADS, C_KV_RANK, C_V), lambda b: (0, 0, 0))],
        out_specs=pl.BlockSpec((N_META, C_HEADS * C_V), lambda b: (b, 0)),
        compiler_params=_params("parallel"),
        name="mla_attn_meta",
    )(qa, kb, et_meta, wuv)


N_GROUPS = N_PAGES // PAGE_GROUP


def _paged_body(pt_ref, q_ref, kown_ref, eown_ref, cache_ref, o_ref, buf, sem, *, layer):
    step = pl.program_id(0)
    nsteps = pl.num_programs(0)
    units = range(PAGED_SEQS)

    def page_copy(st, g, slot, u, i):
        page = pt_ref[st * PAGED_SEQS + u, g * PAGE_GROUP + i]
        return pltpu.make_async_copy(cache_ref.at[layer, page], buf.at[slot, u, i], sem.at[slot, u])

    def start_group(st, g, slot):
        for u in units:
            for i in range(PAGE_GROUP):
                page_copy(st, g, slot, u, i).start()

    def wait_group(st, g, slot):
        for u in units:
            for i in range(PAGE_GROUP):
                page_copy(st, g, slot, u, i).wait()

    @pl.when(step == 0)
    def _():
        start_group(0, 0, 0)

    nt = (((1,), (1,)), ((), ()))
    qs, init = [], []
    for u in units:
        q = q_ref[u]
        kown = kown_ref[u].astype(BF16).astype(F32)
        m0 = jnp.sum(q.astype(F32) * kown, axis=1, keepdims=True) * eown_ref[u]
        qs.append(q)
        init.append((m0, jnp.ones((C_HEADS, 1), F32), jnp.broadcast_to(kown[:, 0:C_KV_RANK], (C_HEADS, C_KV_RANK))))

    def attend(u, slot, m, l, acc):
        s_parts = []
        for i in range(PAGE_GROUP):
            pg = buf[slot, u, i]
            s_i = jnp.dot(qs[u], pg.astype(BF16), preferred_element_type=F32)
            s_parts.append(s_i * jnp.exp(pg[C_KV_RANK + C_ROPE:C_ROW, :]))
        s = jnp.concatenate(s_parts, axis=1)
        m_new = jnp.maximum(m, jnp.max(s, axis=1, keepdims=True))
        alpha = jnp.exp(m - m_new)
        p = jnp.exp(s - m_new).astype(BF16)
        l = alpha * l + jnp.sum(p.astype(F32), axis=1, keepdims=True)
        pv = jnp.zeros((C_HEADS, C_KV_RANK), F32)
        for i in range(PAGE_GROUP):
            c_t = buf[slot, u, i, 0:C_KV_RANK, :].astype(BF16)
            pv = pv + lax.dot_general(p[:, i * PAGE_SIZE:(i + 1) * PAGE_SIZE], c_t, nt, preferred_element_type=F32)
        return m_new, l, alpha * acc + pv

    def group(g, carry):
        slot = g % 2

        @pl.when(g + 1 < N_GROUPS)
        def _():
            start_group(step, g + 1, 1 - slot)

        @pl.when(jnp.logical_and(g + 1 == N_GROUPS, step + 1 < nsteps))
        def _():
            start_group(step + 1, 0, 1 - slot)

        wait_group(step, g, slot)
        return tuple(attend(u, slot, *carry[u]) for u in units)

    final = lax.fori_loop(0, N_GROUPS, group, tuple(init))
    for u in units:
        _, l, acc = final[u]
        o_ref[u] = (acc / l).astype(BF16)


def _paged_attention(page_table, q_s, k_own, e_own, cache_t, layer):
    grid_spec = pltpu.PrefetchScalarGridSpec(
        num_scalar_prefetch=1,
        grid=(DEC_BATCH // PAGED_SEQS,),
        in_specs=[pl.BlockSpec((PAGED_SEQS, C_HEADS, C_ROW), lambda b, pt: (b, 0, 0)),
                  pl.BlockSpec((PAGED_SEQS, 1, C_ROW), lambda b, pt: (b, 0, 0)),
                  pl.BlockSpec((PAGED_SEQS, C_HEADS, 1), lambda b, pt: (b, 0, 0)),
                  pl.BlockSpec(memory_space=pl.ANY)],
        out_specs=pl.BlockSpec((PAGED_SEQS, C_HEADS, C_KV_RANK), lambda b, pt: (b, 0, 0)),
        scratch_shapes=[pltpu.VMEM((2, PAGED_SEQS, PAGE_GROUP, C_ROW, PAGE_SIZE), F32),
                        pltpu.SemaphoreType.DMA((2, PAGED_SEQS))],
    )
    return pl.pallas_call(
        functools.partial(_paged_body, layer=layer),
        out_shape=jax.ShapeDtypeStruct((DEC_BATCH, C_HEADS, C_KV_RANK), BF16),
        grid_spec=grid_spec,
        compiler_params=_params("arbitrary"),
        name="mla_attn_paged",
    )(page_table, q_s, k_own, e_own, cache_t)


def _uv_body(o_ref, w_ref, out_ref):
    out_ref[...] = jnp.dot(o_ref[...], w_ref[...], preferred_element_type=F32).astype(BF16)


def _uv_sample(o_lat_t, wuv):
    return pl.pallas_call(
        _uv_body,
        out_shape=jax.ShapeDtypeStruct((DEC_BATCH, C_HEADS * C_V), BF16),
        grid=(C_HEADS,),
        in_specs=[pl.BlockSpec((None, DEC_BATCH, C_KV_RANK), lambda h: (h, 0, 0)),
                  pl.BlockSpec((None, C_KV_RANK, C_V), lambda h: (h, 0, 0))],
        out_specs=pl.BlockSpec((DEC_BATCH, C_V), lambda h: (0, h)),
        compiler_params=_params("parallel"),
        name="mla_uv_sample",
    )(o_lat_t, wuv)


FFN_TC = 256
FFN_PAD = 8


def _ffn_act(x0, x1, x2, w):
    return w[0:1, :] * x0 + w[1:2, :] * x1 + w[2:3, :] * x2


def _ffn_prompt_body(gm_ref, um_ref, g_ref, u_ref, wg_ref, wu_ref, am_ref, a_ref, gs, us):
    n = N_META + SEQ
    for src_m, src, dst in ((gm_ref, g_ref, gs), (um_ref, u_ref, us)):
        dst[0:FFN_PAD, :] = jnp.zeros((FFN_PAD, FFN_TC), F32)
        dst[FFN_PAD:FFN_PAD + N_META, :] = src_m[...]
        dst[FFN_PAD + N_META:FFN_PAD + n, :] = src[...]

    def conv(s, w_ref):
        w = w_ref[...]
        return _ffn_act(s[FFN_PAD - 2:FFN_PAD - 2 + n, :], s[FFN_PAD - 1:FFN_PAD - 1 + n, :], s[FFN_PAD:FFN_PAD + n, :], w)

    gate = conv(gs, wg_ref)
    up = conv(us, wu_ref)
    act = (gate * jax.nn.sigmoid(gate) * up).astype(BF16)
    am_ref[...] = act[0:N_META, :]
    a_ref[...] = act[N_META:n, :]


def _ffn_prompt(hup, w_conv):
    meta_blk = META0 // N_META
    nj = D_FF // FFN_TC
    return pl.pallas_call(
        _ffn_prompt_body,
        out_shape=[jax.ShapeDtypeStruct((BATCH * N_META, D_FF), BF16),
                   jax.ShapeDtypeStruct((N_REAL, D_FF), BF16)],
        grid=(BATCH, nj),
        in_specs=[pl.BlockSpec((N_META, FFN_TC), lambda b, j: (meta_blk + b, j)),
                  pl.BlockSpec((N_META, FFN_TC), lambda b, j: (meta_blk + b, nj + j)),
                  pl.BlockSpec((SEQ, FFN_TC), lambda b, j: (b, j)),
                  pl.BlockSpec((SEQ, FFN_TC), lambda b, j: (b, nj + j)),
                  pl.BlockSpec((FFN_CONV, FFN_TC), lambda b, j: (0, j)),
                  pl.BlockSpec((FFN_CONV, FFN_TC), lambda b, j: (0, nj + j))],
        out_specs=[pl.BlockSpec((N_META, FFN_TC), lambda b, j: (b, j)),
                   pl.BlockSpec((SEQ, FFN_TC), lambda b, j: (b, j))],
        scratch_shapes=[pltpu.VMEM((FFN_PAD + N_META + SEQ, FFN_TC), F32),
                        pltpu.VMEM((FFN_PAD + N_META + SEQ, FFN_TC), F32)],
        compiler_params=_params("parallel", "parallel"),
        name="ffn_act_prompt",
    )(hup, hup, hup, hup, w_conv, w_conv)


FFN_TS = 1024


def _ffn_sample_body(g_ref, u_ref, g2_ref, u2_ref, g1_ref, u1_ref, wg_ref, wu_ref, a_ref):
    gate = _ffn_act(g2_ref[...], g1_ref[...], g_ref[...], wg_ref[...])
    up = _ffn_act(u2_ref[...], u1_ref[...], u_ref[...], wu_ref[...])
    a_ref[...] = (gate * jax.nn.sigmoid(gate) * up).astype(BF16)


def _ffn_sample(hup, prev2, prev1, w_conv):
    nj = D_FF // FFN_TS
    rb = SAMP0 // DEC_BATCH

    def st(arr_is_up):
        return pl.BlockSpec((DEC_BATCH, FFN_TS), (lambda j: (0, nj + j)) if arr_is_up else (lambda j: (0, j)))

    return pl.pallas_call(
        _ffn_sample_body,
        out_shape=jax.ShapeDtypeStruct((DEC_BATCH, D_FF), BF16),
        grid=(nj,),
        in_specs=[pl.BlockSpec((DEC_BATCH, FFN_TS), lambda j: (rb, j)),
                  pl.BlockSpec((DEC_BATCH, FFN_TS), lambda j: (rb, nj + j)),
                  st(False), st(True), st(False), st(True),
                  pl.BlockSpec((FFN_CONV, FFN_TS), lambda j: (0, j)),
                  pl.BlockSpec((FFN_CONV, FFN_TS), lambda j: (0, nj + j))],
        out_specs=pl.BlockSpec((DEC_BATCH, FFN_TS), lambda j: (0, j)),
        compiler_params=_params("parallel"),
        name="ffn_act_sample",
    )(hup, hup, prev2, prev2, prev1, prev1, w_conv, w_conv)


def _join_rows(real, meta, samp):
    pad = jnp.zeros((SAMP0 - N_PROMPT, real.shape[1]), real.dtype)
    return jnp.concatenate([real, meta, pad, samp], axis=0)


def _rope_tables():
    pos = np.zeros((MP,), np.float32)
    pos[:N_REAL] = np.tile(N_META + np.arange(SEQ), BATCH)
    pos[META0:N_PROMPT] = np.tile(np.arange(N_META), BATCH)
    pos[SAMP0:] = PAST_LEN
    half = C_ROPE // 2
    inv_freq = ROPE_THETA ** (-jnp.arange(half, dtype=F32) / half)
    ang = jnp.asarray(pos)[:, None] * inv_freq[None, :]
    cos, sin = jnp.cos(ang), jnp.sin(ang)
    zeros = jnp.zeros((MP, 128 - C_ROPE), F32)
    return jnp.concatenate([cos, cos, zeros], axis=1), jnp.concatenate([-sin, sin, zeros], axis=1)


def _prompt_rows(arr, width):
    real = arr[:N_REAL].reshape(BATCH, SEQ, width)
    meta = arr[META0:N_PROMPT].reshape(BATCH, N_META, width)
    return jnp.concatenate([meta, real], axis=1)


def _layer(x, l, cos_t, sin_t, cache_t, page_table, st_c, st_n, st_m, st_conv, st_ffn, p):
    w_in = p["w_in"][l]
    sl = lambda a, b: w_in[:, a:b]
    w_in_r = jnp.concatenate(
        [sl(1088, 5184), sl(5192, 7240), sl(7240, 19528), sl(0, 1088), sl(5184, 5192),
         jnp.zeros((D_MODEL, NY - 19528), F32)], axis=1).astype(BF16)
    xn = _rms_norm(x, p["g_mix"][l], width=D_MODEL, col_off=0, tm=352, name="norm_mix")
    y = _matmul(xn, w_in_r, tm=1056, tn=512, name="proj_in")

    bias = jnp.concatenate([p["b_i"][l], p["b_f"][l]]).astype(F32)
    gcols = y[:, Y_SMALL + LANE_AI:Y_SMALL + LANE_AI + 2 * A_HEADS]
    gt_real = gcols[:N_REAL].reshape(BATCH, SEQ // A_CHUNK, A_CHUNK, 2 * A_HEADS).transpose(0, 3, 1, 2)
    gt_meta = gcols[META0:N_PROMPT].reshape(BATCH, N_META, 2 * A_HEADS).transpose(0, 2, 1)[:, :, None, :]
    gnorm = p["g_a_norm"][l].reshape(A_HEADS, 1, A_HEAD_DIM)
    ha_m, ha_r, c_p, n_p, m_p = _mlstm_prompt(y, gt_meta, gt_real, bias, gnorm)
    ys = y[SAMP0:]
    kt = ys[:, Y_AK:Y_AK + A_WIDTH].reshape(DEC_BATCH, A_HEADS, A_HEAD_DIM).transpose(1, 2, 0)
    qt = ys[:, Y_AQ:Y_AQ + A_WIDTH].reshape(DEC_BATCH, A_HEADS, A_HEAD_DIM).transpose(1, 2, 0)
    ha_s, c_s, n_s_t, m_s_t = _mlstm_sample(
        y, kt, qt, st_c, st_n[l].transpose(1, 0, 2), st_m[l].T[:, :, None], bias, gnorm, l)
    ha = _join_rows(ha_r, ha_m, ha_s)
    mixed = _matmul(ha, p["w_a_out"], layer=l, tm=1056, tn=1024, gate=y, gate_off=Y_GA, name="proj_a")

    cb = p["b_b_conv"][l].reshape(1, B_WIDTH)
    lg = p["g_b_ln"][l].reshape(1, B_WIDTH)
    lb = p["b_b_ln"][l].reshape(1, B_WIDTH)
    hb_m, hb_r, conv_p = _conv_prompt(y, p["w_b_conv"][l], cb, lg, lb)
    hb_s, conv_s = _conv_sample(y, st_conv[l], p["w_b_conv"][l], cb, lg, lb)
    hb = _join_rows(hb_r, hb_m, hb_s)
    mixed = _matmul(hb, p["w_b_out"], layer=l, tm=1056, tn=1024, gate=y, gate_off=Y_GB, prev=mixed, name="proj_b")

    w_q = p["w_q_b"][l].reshape(C_Q_RANK, C_HEADS, C_QK)
    w_q = jnp.pad(w_q, ((0, 0), (0, 0), (0, QH - C_QK))).reshape(C_Q_RANK, C_HEADS * QH).astype(BF16)
    cqn = _rms_norm(y, p["g_cq"][l], width=C_Q_RANK, col_off=Y_CQ, tm=1056, name="norm_cq")
    q = _matmul(cqn, w_q, tm=1056, tn=1024, name="proj_q")
    gain = p["g_qn"][l] * (p["g_kn"][l] * C_QK ** -0.5)
    gain_n = gain[:C_NOPE].reshape(1, C_NOPE)
    gain_r = jnp.pad(gain[C_NOPE:], (0, 128 - C_ROPE)).reshape(1, 128)
    w_uk = p["w_uk"][l]
    qa = _qpost(q, cos_t, sin_t, gain_n, gain_r, w_uk.transpose(1, 2, 0).astype(BF16), tm=352)
    rows, kb, e = _rows(y, cos_t, sin_t, p["g_ckv"][l].reshape(1, C_KV_RANK),
                        w_uk.reshape(C_KV_RANK, C_HEADS * C_NOPE).astype(BF16), tm=352)
    wuv = p["w_uv"][l].transpose(1, 0, 2).astype(BF16)
    et = e[:, :C_HEADS].T
    et_real = et[:, None, :N_REAL]
    et_meta = et[:, META0:N_PROMPT].reshape(C_HEADS, BATCH, 1, N_META)
    o_r = _attn_real(qa, kb, et_real, et_meta, wuv)
    o_m = _attn_meta(qa, kb, et_meta, wuv)
    q_s = qa[:, SAMP0:, :C_ROW].transpose(1, 0, 2)
    q_s = q_s.at[:, :, C_KV_RANK + C_ROPE:].set(0)
    o_lat_s = _paged_attention(page_table, q_s, rows[SAMP0:, None, :], e[SAMP0:, :C_HEADS, None], cache_t, l)
    o_s = _uv_sample(o_lat_s.transpose(1, 0, 2), wuv)
    o = _join_rows(o_r, o_m, o_s)
    mixed = _matmul(o, p["w_c_out"], layer=l, tm=1056, tn=1024, gate=y, gate_off=Y_GC, prev=mixed,
                    out_dtype=BF16, name="proj_c")

    x = _matmul(mixed, p["w_out"], layer=l, tm=1056, tn=512, resid=x, name="proj_out")

    xn2 = _rms_norm(x, p["g_ffn"][l], width=D_MODEL, col_off=0, tm=352, name="norm_ffn")
    hup = _matmul(xn2, p["w_up"], layer=l, tm=1056, tn=1024, name="proj_up")
    act_m, act_r = _ffn_prompt(hup, p["w_ffn_conv"][l])
    act_s = _ffn_sample(hup, st_ffn[l][:, 0, :], st_ffn[l][:, 1, :], p["w_ffn_conv"][l])
    act = _join_rows(act_r, act_m, act_s)
    x = _matmul(act, p["w_down"], layer=l, tm=1056, tn=512, tk=4096, resid=x, name="proj_down")

    hup_s = hup[SAMP0:]
    ffn_p = jnp.stack([hup[b * SEQ + SEQ - (FFN_CONV - 1):(b + 1) * SEQ] for b in range(BATCH)])
    outs = dict(
        rows_p=_prompt_rows(rows, C_ROW), rows_s=rows[SAMP0:, None, :],
        c_p=c_p, c_s=c_s,
        n_p=n_p.reshape(BATCH, A_HEADS, A_HEAD_DIM), n_s=n_s_t.transpose(1, 0, 2),
        m_p=m_p.reshape(BATCH, A_HEADS), m_s=m_s_t[:, :, 0].T,
        conv_p=conv_p, conv_s=conv_s,
        ffn_p=ffn_p,
        ffn_s=jnp.stack([st_ffn[l][:, 1, :], hup_s], axis=1),
    )
    return x, outs


def kernel(x_prompt, x_sample, cache_mla, page_table, state_mlstm_c, state_mlstm_n, state_mlstm_m, state_conv, state_ffn_conv, meta_tokens, g_mix, w_in, b_i, b_f, g_a_norm, w_a_out, w_b_conv, b_b_conv, g_b_ln, b_b_ln, w_b_out, g_cq, w_q_b, g_ckv, w_uk, w_uv, g_qn, g_kn, w_c_out, w_out, g_ffn, w_up, w_ffn_conv, w_down):
    p = dict(g_mix=g_mix, w_in=w_in, b_i=b_i, b_f=b_f, g_a_norm=g_a_norm, w_a_out=w_a_out, w_b_conv=w_b_conv,
             b_b_conv=b_b_conv, g_b_ln=g_b_ln, b_b_ln=b_b_ln, w_b_out=w_b_out, g_cq=g_cq, w_q_b=w_q_b,
             g_ckv=g_ckv, w_uk=w_uk, w_uv=w_uv, g_qn=g_qn, g_kn=g_kn, w_c_out=w_c_out, w_out=w_out,
             g_ffn=g_ffn, w_up=w_up, w_ffn_conv=w_ffn_conv, w_down=w_down)
    for name in ("w_a_out", "w_b_out", "w_c_out", "w_out", "w_up", "w_down"):
        p[name] = p[name].astype(BF16)
    cache_t = jnp.swapaxes(cache_mla, 2, 3)
    x = _join_rows(x_prompt.reshape(N_REAL, D_MODEL),
                   jnp.tile(meta_tokens.astype(x_prompt.dtype), (BATCH, 1)),
                   x_sample.reshape(DEC_BATCH, D_MODEL))
    cos_t, sin_t = _rope_tables()
    per_layer = []
    for l in range(DEPTH):
        x, outs = _layer(x, l, cos_t, sin_t, cache_t, page_table, state_mlstm_c, state_mlstm_n, state_mlstm_m,
                         state_conv, state_ffn_conv, p)
        per_layer.append(outs)
    stack = lambda name: jnp.stack([o[name] for o in per_layer])
    return (x[:N_REAL].reshape(BATCH, SEQ, D_MODEL), x[SAMP0:].reshape(DEC_BATCH, 1, D_MODEL),
            stack("rows_p"), stack("rows_s"), stack("c_p"), stack("c_s"), stack("n_p"), stack("n_s"),
            stack("m_p"), stack("m_s"), stack("conv_p"), stack("conv_s"), stack("ffn_p"), stack("ffn_s"))
```

```python
import functools

import jax
import jax.numpy as jnp
import numpy as np
from jax import lax
from jax.experimental import pallas as pl
from jax.experimental.pallas import tpu as pltpu

F32 = jnp.float32
BF16 = jnp.bfloat16

D_MODEL = 4096
BATCH = 4
SEQ = 2048
DEPTH = 2
DEC_BATCH = 128
PAST_LEN = 16384
PAGE_SIZE = 128
N_PAGES = PAST_LEN // PAGE_SIZE
N_META = 16
A_HEADS = 4
A_HEAD_DIM = 256
A_WIDTH = A_HEADS * A_HEAD_DIM
A_CHUNK = 64
B_WIDTH = 1024
B_CONV = 31
C_HEADS = 16
C_NOPE = 128
C_ROPE = 64
C_QK = C_NOPE + C_ROPE
C_V = 128
C_Q_RANK = 768
C_KV_RANK = 256
C_ROW = C_KV_RANK + C_ROPE + C_HEADS
ROPE_THETA = 10000.0
D_FF = 8192
FFN_CONV = 3
EPS = 1e-6

N_REAL = BATCH * SEQ
META0 = N_REAL
N_PROMPT = N_REAL + BATCH * N_META
SAMP0 = 8320
MP = SAMP0 + DEC_BATCH

Y_AQ, Y_AK, Y_AV, Y_AO = 0, 1024, 2048, 3072
Y_BU, Y_BG = 4096, 5120
Y_GA, Y_GB, Y_GC = 6144, 10240, 14336
Y_CQ, Y_CKV, Y_SMALL, Y_GATE = 18432, 19200, 19456, 19968
IN_TN = 512
IN_SRC = ([1088 + IN_TN * t for t in range(8)] + [5192 + IN_TN * t for t in range(4)]
          + [7240 + IN_TN * t for t in range(24)] + [0, 512, 1024, 5184])
NY = IN_TN * len(IN_SRC)
LANE_AI, LANE_AF = 0, 4

QH = 256
KW = 384

VMEM_LIMIT = 56 * 1024 * 1024
PAGE_GROUP = 8
PAGED_SEQS = 2
PAGE_SLOTS = 4


def _params(*sem):
    return pltpu.CompilerParams(dimension_semantics=sem, vmem_limit_bytes=VMEM_LIMIT)


def _mm_body(*refs, nk, mode, has_prev):
    x_ref, w_ref = refs[0], refs[1]
    idx = 2
    r_ref = g_ref = p_ref = None
    if mode == "resid":
        r_ref = refs[idx]; idx += 1
    if mode == "gate":
        g_ref = refs[idx]; idx += 1
        if has_prev:
            p_ref = refs[idx]; idx += 1
    o_ref = refs[idx]; idx += 1
    acc_ref = refs[idx] if nk > 1 else None

    def epilogue(acc):
        if mode == "resid":
            acc = r_ref[...] + acc
        elif mode == "gate":
            acc = jax.nn.sigmoid(g_ref[...]) * acc
            if p_ref is not None:
                acc = p_ref[...] + acc
        o_ref[...] = acc.astype(o_ref.dtype)

    part = jnp.dot(x_ref[...], w_ref[...].astype(BF16), preferred_element_type=F32)
    if nk == 1:
        epilogue(part)
    else:
        k = pl.program_id(2)

        @pl.when(k == 0)
        def _():
            acc_ref[...] = part

        @pl.when(jnp.logical_and(k > 0, k < nk - 1))
        def _():
            acc_ref[...] += part

        @pl.when(k == nk - 1)
        def _():
            epilogue(acc_ref[...] + part)


def _matmul(x, w, *, tm, tn, tk=None, out_dtype=F32, resid=None, gate=None, gate_off=0, prev=None, layer=None, name):
    m, kdim = x.shape
    n = w.shape[-1]
    tk = kdim if tk is None else tk
    nk = kdim // tk
    assert m % tm == 0 and n % tn == 0 and kdim % tk == 0 and gate_off % tn == 0
    mode = "resid" if resid is not None else ("gate" if gate is not None else "plain")
    if w.ndim == 3:
        w_spec = pl.BlockSpec((None, tk, tn), lambda i, j, k: (layer, k, j))
    else:
        w_spec = pl.BlockSpec((tk, tn), lambda i, j, k: (k, j))
    in_specs = [pl.BlockSpec((tm, tk), lambda i, j, k: (i, k)), w_spec]
    args = [x, w]
    if resid is not None:
        in_specs.append(pl.BlockSpec((tm, tn), lambda i, j, k: (i, j)))
        args.append(resid)
    if gate is not None:
        goff = gate_off // tn
        in_specs.append(pl.BlockSpec((tm, tn), lambda i, j, k: (i, goff + j)))
        args.append(gate)
        if prev is not None:
            in_specs.append(pl.BlockSpec((tm, tn), lambda i, j, k: (i, j)))
            args.append(prev)
    return pl.pallas_call(
        functools.partial(_mm_body, nk=nk, mode=mode, has_prev=prev is not None),
        out_shape=jax.ShapeDtypeStruct((m, n), out_dtype),
        grid=(m // tm, n // tn, nk),
        in_specs=in_specs,
        out_specs=pl.BlockSpec((tm, tn), lambda i, j, k: (i, j)),
        scratch_shapes=[pltpu.VMEM((tm, tn), F32)] if nk > 1 else [],
        compiler_params=_params("parallel", "parallel", "arbitrary"),
        name=name,
    )(*args)


def _proj_in_body(src_ref, x_ref, w_ref, o_ref):
    del src_ref
    w = w_ref[0].astype(BF16)
    o_ref[...] = lax.dot_general(x_ref[...], w, (((1,), (1,)), ((), ())), preferred_element_type=F32)


def _proj_in(xn, w_in_t, layer, *, tm):
    src = jnp.asarray(IN_SRC, jnp.int32)
    grid_spec = pltpu.PrefetchScalarGridSpec(
        num_scalar_prefetch=1,
        grid=(MP // tm, len(IN_SRC)),
        in_specs=[pl.BlockSpec((tm, D_MODEL), lambda i, j, src: (i, 0)),
                  pl.BlockSpec((pl.Element(1), pl.Element(IN_TN), pl.Element(D_MODEL)),
                               lambda i, j, src: (layer, pl.multiple_of(src[j], 8), 0))],
        out_specs=pl.BlockSpec((tm, IN_TN), lambda i, j, src: (i, j)),
    )
    return pl.pallas_call(
        _proj_in_body,
        out_shape=jax.ShapeDtypeStruct((MP, NY), F32),
        grid_spec=grid_spec,
        compiler_params=_params("parallel", "arbitrary"),
        name="proj_in",
    )(src, xn, w_in_t)


def _rms_body(x_ref, g_ref, o_ref):
    x = x_ref[...]
    ms = jnp.mean(x * x, axis=-1, keepdims=True)
    o_ref[...] = (x * lax.rsqrt(ms + EPS) * g_ref[...]).astype(o_ref.dtype)


def _rms_norm(x, g, *, width, col_off, tm, name):
    m = x.shape[0]
    cblk = col_off // width
    return pl.pallas_call(
        _rms_body,
        out_shape=jax.ShapeDtypeStruct((m, width), BF16),
        grid=(m // tm,),
        in_specs=[pl.BlockSpec((tm, width), lambda i: (i, cblk)),
                  pl.BlockSpec((1, width), lambda i: (0, 0))],
        out_specs=pl.BlockSpec((tm, width), lambda i: (i, 0)),
        compiler_params=_params("parallel"),
        name=name,
    )(x, g.reshape(1, width))


def _lane_pick(g, lane_idx):
    lane = lax.broadcasted_iota(jnp.int32, g.shape, 1)
    return jnp.sum(jnp.where(lane == lane_idx, g, 0.0), axis=1, keepdims=True)


def _head_norm_gate(hh, gn, ao):
    ms = jnp.mean(hh * hh, axis=-1, keepdims=True)
    return (hh * lax.rsqrt(ms + EPS) * gn * jax.nn.sigmoid(ao)).astype(BF16)


def _mlstm_prompt_body(bias_ref, qm_ref, km_ref, vm_ref, aom_ref, gm_ref, irm_ref, frm_ref,
                       q_ref, k_ref, v_ref, ao_ref, g_ref, irr_ref, frr_ref, gn_ref,
                       hm_ref, h_ref, c_ref, n_ref, m_ref):
    h = pl.program_id(1)
    b_i = bias_ref[h]
    b_f = bias_ref[A_HEADS + h]
    gn = gn_ref[...]

    def chunk(length, q, k, v, ao, g, ai_row, af_row, n_prev, m_prev):
        k = k * (A_HEAD_DIM ** -0.5)
        it_col = _lane_pick(g, LANE_AI + h) + b_i
        lf_col = jax.nn.log_sigmoid(_lane_pick(g, LANE_AF + h) + b_f)
        it_row = ai_row + b_i
        lf_row = jax.nn.log_sigmoid(af_row + b_f)
        t_i = lax.broadcasted_iota(jnp.int32, (length, length), 0)
        s_i = lax.broadcasted_iota(jnp.int32, (length, length), 1)
        causal = s_i <= t_i
        b_col = jnp.sum(jnp.where(causal, lf_row, 0.0), axis=1, keepdims=True)
        b_row = jnp.sum(jnp.where(t_i <= s_i, lf_col, 0.0), axis=0, keepdims=True)
        log_d = jnp.where(causal, b_col - b_row + it_row, -jnp.inf)
        log_inter = b_col + m_prev
        m_t = jnp.maximum(log_inter, jnp.max(log_d, axis=1, keepdims=True))
        d = jnp.exp(log_d - m_t)
        w_inter = jnp.exp(log_inter - m_t)
        qb = q.astype(BF16)
        kb = k.astype(BF16)
        vb = v.astype(BF16)
        c_prev = c_ref[...]
        s = lax.dot_general(qb, kb, (((1,), (1,)), ((), ())), preferred_element_type=F32) * d
        num = (w_inter * jnp.dot(qb, c_prev.astype(BF16), preferred_element_type=F32)
               + jnp.dot(s.astype(BF16), vb, preferred_element_type=F32))
        den = w_inter * jnp.sum(q * n_prev, axis=1, keepdims=True) + jnp.sum(s, axis=1, keepdims=True)
        hh = num / jnp.maximum(jnp.abs(den), jnp.exp(-m_t))
        m_new = m_t[length - 1:length, :]
        b_last = b_col[length - 1:length, :]
        w_end = jnp.exp(b_last - b_col + it_col - m_new)
        decay = jnp.exp(b_last + m_prev - m_new)
        kw = k * w_end
        c_ref[...] = decay * c_prev + lax.dot_general(
            kw.astype(BF16), vb, (((0,), (0,)), ((), ())), preferred_element_type=F32)
        n_new = decay * n_prev + jnp.sum(kw, axis=0, keepdims=True)
        return _head_norm_gate(hh, gn, ao), n_new, m_new

    c_ref[...] = jnp.zeros_like(c_ref)
    n0 = jnp.zeros((1, A_HEAD_DIM), F32)
    m0 = jnp.zeros((1, 1), F32)
    out, n1, m1 = chunk(N_META, qm_ref[...], km_ref[...], vm_ref[...], aom_ref[...], gm_ref[...],
                        irm_ref[...], frm_ref[...], n0, m0)
    hm_ref[...] = out

    def step(i, carry):
        n_prev, m_prev = carry
        r0 = pl.multiple_of(i * A_CHUNK, A_CHUNK)
        rows = pl.ds(r0, A_CHUNK)
        out, n_new, m_new = chunk(A_CHUNK, q_ref[rows, :], k_ref[rows, :], v_ref[rows, :], ao_ref[rows, :],
                                  g_ref[rows, :], irr_ref[pl.ds(i, 1), :], frr_ref[pl.ds(i, 1), :],
                                  n_prev, m_prev)
        h_ref[rows, :] = out
        return n_new, m_new

    n_fin, m_fin = lax.fori_loop(0, SEQ // A_CHUNK, step, (n1, m1))
    n_ref[...] = n_fin
    m_ref[...] = m_fin


def _mlstm_prompt(y, gt_meta, gt_real, bias, gnorm):
    meta_blk = META0 // N_META

    def ycol(off):
        return off // A_HEAD_DIM

    def meta_spec(off):
        return pl.BlockSpec((N_META, A_HEAD_DIM), lambda b, h: (meta_blk + b, ycol(off) + h))

    def real_spec(off):
        return pl.BlockSpec((SEQ, A_HEAD_DIM), lambda b, h: (b, ycol(off) + h))

    in_specs = [
        pl.BlockSpec(memory_space=pltpu.SMEM),
        meta_spec(Y_AQ), meta_spec(Y_AK), meta_spec(Y_AV), meta_spec(Y_AO),
        pl.BlockSpec((N_META, 128), lambda b, h: (meta_blk + b, Y_GATE // 128)),
        pl.BlockSpec((None, None, 1, N_META), lambda b, h: (b, h, 0, 0)),
        pl.BlockSpec((None, None, 1, N_META), lambda b, h: (b, A_HEADS + h, 0, 0)),
        real_spec(Y_AQ), real_spec(Y_AK), real_spec(Y_AV), real_spec(Y_AO),
        pl.BlockSpec((SEQ, 128), lambda b, h: (b, Y_GATE // 128)),
        pl.BlockSpec((None, None, SEQ // A_CHUNK, A_CHUNK), lambda b, h: (b, h, 0, 0)),
        pl.BlockSpec((None, None, SEQ // A_CHUNK, A_CHUNK), lambda b, h: (b, A_HEADS + h, 0, 0)),
        pl.BlockSpec((None, 1, A_HEAD_DIM), lambda b, h: (h, 0, 0)),
    ]
    out_shape = [
        jax.ShapeDtypeStruct((BATCH * N_META, A_WIDTH), BF16),
        jax.ShapeDtypeStruct((N_REAL, A_WIDTH), BF16),
        jax.ShapeDtypeStruct((BATCH, A_HEADS, A_HEAD_DIM, A_HEAD_DIM), F32),
        jax.ShapeDtypeStruct((BATCH, A_HEADS, 1, A_HEAD_DIM), F32),
        jax.ShapeDtypeStruct((BATCH, A_HEADS, 1, 1), F32),
    ]
    out_specs = [
        pl.BlockSpec((N_META, A_HEAD_DIM), lambda b, h: (b, h)),
        pl.BlockSpec((SEQ, A_HEAD_DIM), lambda b, h: (b, h)),
        pl.BlockSpec((None, None, A_HEAD_DIM, A_HEAD_DIM), lambda b, h: (b, h, 0, 0)),
        pl.BlockSpec((None, None, 1, A_HEAD_DIM), lambda b, h: (b, h, 0, 0)),
        pl.BlockSpec((None, None, 1, 1), lambda b, h: (b, h, 0, 0)),
    ]
    return pl.pallas_call(
        _mlstm_prompt_body,
        out_shape=out_shape,
        grid=(BATCH, A_HEADS),
        in_specs=in_specs,
        out_specs=out_specs,
        compiler_params=_params("parallel", "parallel"),
        name="mlstm_prompt",
    )(bias, y, y, y, y, y, gt_meta, gt_meta, y, y, y, y, y, gt_real, gt_real, gnorm)


MS_NB = 8


def _mlstm_sample_body(bias_ref, q_ref, k_ref, v_ref, ao_ref, g_ref, kt_ref, qt_ref, c_ref, n_ref, m_ref, gn_ref,
                       h_ref, co_ref, no_ref, mo_ref):
    h = pl.program_id(0)
    bt = pl.program_id(1)
    b_i = bias_ref[h]
    b_f = bias_ref[A_HEADS + h]
    q = q_ref[...]
    k = k_ref[...] * (A_HEAD_DIM ** -0.5)
    v = v_ref[...]
    g = g_ref[...]
    it = _lane_pick(g, LANE_AI + h) + b_i
    lf = jax.nn.log_sigmoid(_lane_pick(g, LANE_AF + h) + b_f)
    m_prev = m_ref[...]
    n_prev = n_ref[...]
    log_inter = lf + m_prev
    m_t = jnp.maximum(log_inter, it)
    d = jnp.exp(it - m_t)
    w_inter = jnp.exp(log_inter - m_t)
    s = jnp.sum(q * k, axis=1, keepdims=True) * d
    den = w_inter * jnp.sum(q * n_prev, axis=1, keepdims=True) + s
    wv = d * v
    lane = lax.broadcasted_iota(jnp.int32, (A_HEAD_DIM, DEC_BATCH), 1)
    kt = kt_ref[...] * (A_HEAD_DIM ** -0.5)
    qt = qt_ref[...]
    qc_rows = []
    for j in range(MS_NB):
        sel = lane == bt * MS_NB + j
        k_col = jnp.sum(jnp.where(sel, kt, 0.0), axis=1, keepdims=True)
        q_col = jnp.sum(jnp.where(sel, qt, 0.0), axis=1, keepdims=True)
        c_prev = c_ref[j]
        qc_rows.append(jnp.sum(q_col * c_prev, axis=0, keepdims=True))
        co_ref[j] = w_inter[j:j + 1, :] * c_prev + k_col * wv[j:j + 1, :]
    qc = jnp.concatenate(qc_rows, axis=0)
    num = w_inter * qc + s * v
    hh = num / jnp.maximum(jnp.abs(den), jnp.exp(-m_t))
    h_ref[...] = _head_norm_gate(hh, gn_ref[...], ao_ref[...])
    no_ref[...] = w_inter * n_prev + d * k
    mo_ref[...] = m_t


def _mlstm_sample(y, kt, qt, c_state, n_state_t, m_state_t, bias, gnorm, layer):
    rb0 = SAMP0 // MS_NB

    def yspec(off):
        return pl.BlockSpec((MS_NB, A_HEAD_DIM), lambda h, bt: (rb0 + bt, off // A_HEAD_DIM + h))

    in_specs = [
        pl.BlockSpec(memory_space=pltpu.SMEM),
        yspec(Y_AQ), yspec(Y_AK), yspec(Y_AV), yspec(Y_AO),
        pl.BlockSpec((MS_NB, 128), lambda h, bt: (rb0 + bt, Y_GATE // 128)),
        pl.BlockSpec((None, A_HEAD_DIM, DEC_BATCH), lambda h, bt: (h, 0, 0)),
        pl.BlockSpec((None, A_HEAD_DIM, DEC_BATCH), lambda h, bt: (h, 0, 0)),
        pl.BlockSpec((None, MS_NB, None, A_HEAD_DIM, A_HEAD_DIM), lambda h, bt: (layer, bt, h, 0, 0)),
        pl.BlockSpec((None, MS_NB, A_HEAD_DIM), lambda h, bt: (h, bt, 0)),
        pl.BlockSpec((None, MS_NB, 1), lambda h, bt: (h, bt, 0)),
        pl.BlockSpec((None, 1, A_HEAD_DIM), lambda h, bt: (h, 0, 0)),
    ]
    out_shape = [
        jax.ShapeDtypeStruct((DEC_BATCH, A_WIDTH), BF16),
        jax.ShapeDtypeStruct((DEC_BATCH, A_HEADS, A_HEAD_DIM, A_HEAD_DIM), F32),
        jax.ShapeDtypeStruct((A_HEADS, DEC_BATCH, A_HEAD_DIM), F32),
        jax.ShapeDtypeStruct((A_HEADS, DEC_BATCH, 1), F32),
    ]
    out_specs = [
        pl.BlockSpec((MS_NB, A_HEAD_DIM), lambda h, bt: (bt, h)),
        pl.BlockSpec((MS_NB, None, A_HEAD_DIM, A_HEAD_DIM), lambda h, bt: (bt, h, 0, 0)),
        pl.BlockSpec((None, MS_NB, A_HEAD_DIM), lambda h, bt: (h, bt, 0)),
        pl.BlockSpec((None, MS_NB, 1), lambda h, bt: (h, bt, 0)),
    ]
    return pl.pallas_call(
        _mlstm_sample_body,
        out_shape=out_shape,
        grid=(A_HEADS, DEC_BATCH // MS_NB),
        in_specs=in_specs,
        out_specs=out_specs,
        compiler_params=_params("parallel", "parallel"),
        name="mlstm_sample",
    )(bias, y, y, y, y, y, kt, qt, c_state, n_state_t, m_state_t, gnorm)


CONV_PAD = 32
CONV_ROWS = 32


def _ln_swish(u, bias, g, b):
    u = u + bias
    mu = jnp.mean(u, axis=-1, keepdims=True)
    var = jnp.mean(jnp.square(u - mu), axis=-1, keepdims=True)
    y = (u - mu) * lax.rsqrt(var + EPS) * g + b
    return (y * jax.nn.sigmoid(y)).astype(BF16)


def _conv_prompt_body(bum_ref, bgm_ref, bu_ref, bg_ref, w_ref, cb_ref, lg_ref, lb_ref,
                      hm_ref, h_ref, st_ref, u_s):
    seq0 = CONV_PAD + N_META
    u_s[0:CONV_PAD, :] = jnp.zeros((CONV_PAD, B_WIDTH), F32)
    u_s[CONV_PAD:seq0, :] = bum_ref[...] * jax.nn.sigmoid(bgm_ref[...])

    def glu(i, _):
        rows = pl.ds(pl.multiple_of(i * 256, 256), 256)
        u_s[pl.ds(pl.multiple_of(seq0 + i * 256, 16), 256), :] = bu_ref[rows, :] * jax.nn.sigmoid(bg_ref[rows, :])
        return 0

    lax.fori_loop(0, SEQ // 256, glu, 0)
    bias, lg, lb = cb_ref[...], lg_ref[...], lb_ref[...]

    def conv_rows(start, nrows):
        win = u_s[pl.ds(start, nrows + CONV_PAD), :]
        acc = jnp.zeros((nrows, B_WIDTH), F32)
        for b in range(8):
            taps = range(b, B_CONV, 8)
            shifted = win[2 + b:2 + b + nrows + taps[-1] - b, :]
            for k in taps:
                acc = acc + w_ref[k:k + 1, :] * shifted[k - b:k - b + nrows, :]
        return _ln_swish(acc, bias, lg, lb)

    hm_ref[...] = conv_rows(0, N_META)

    def step(i, _):
        r0 = pl.multiple_of(i * CONV_ROWS, CONV_ROWS)
        h_ref[pl.ds(r0, CONV_ROWS), :] = conv_rows(pl.multiple_of(N_META + r0, 8), CONV_ROWS)
        return 0

    lax.fori_loop(0, SEQ // CONV_ROWS, step, 0)
    st_ref[...] = u_s[CONV_PAD + N_META + SEQ - (B_CONV - 1):CONV_PAD + N_META + SEQ, :]


def _conv_prompt(y, w, cb, lg, lb):
    meta_blk = META0 // N_META
    cu, cg = Y_BU // B_WIDTH, Y_BG // B_WIDTH
    vec = pl.BlockSpec((1, B_WIDTH), lambda b: (0, 0))
    return pl.pallas_call(
        _conv_prompt_body,
        out_shape=[jax.ShapeDtypeStruct((BATCH * N_META, B_WIDTH), BF16),
                   jax.ShapeDtypeStruct((N_REAL, B_WIDTH), BF16),
                   jax.ShapeDtypeStruct((BATCH, B_CONV - 1, B_WIDTH), F32)],
        grid=(BATCH,),
        in_specs=[pl.BlockSpec((N_META, B_WIDTH), lambda b: (meta_blk + b, cu)),
                  pl.BlockSpec((N_META, B_WIDTH), lambda b: (meta_blk + b, cg)),
                  pl.BlockSpec((SEQ, B_WIDTH), lambda b: (b, cu)),
                  pl.BlockSpec((SEQ, B_WIDTH), lambda b: (b, cg)),
                  pl.BlockSpec((B_CONV, B_WIDTH), lambda b: (0, 0)),
                  vec, vec, vec],
        out_specs=[pl.BlockSpec((N_META, B_WIDTH), lambda b: (b, 0)),
                   pl.BlockSpec((SEQ, B_WIDTH), lambda b: (b, 0)),
                   pl.BlockSpec((None, B_CONV - 1, B_WIDTH), lambda b: (b, 0, 0))],
        scratch_shapes=[pltpu.VMEM((CONV_PAD + N_META + SEQ, B_WIDTH), F32)],
        compiler_params=_params("parallel"),
        name="conv_prompt",
    )(y, y, y, y, w, cb, lg, lb)


CS_NB = 8


def _conv_sample_body(bu_ref, bg_ref, st_ref, w_ref, cb_ref, lg_ref, lb_ref, h_ref, so_ref):
    u = bu_ref[...] * jax.nn.sigmoid(bg_ref[...])
    w_hist = w_ref[0:B_CONV - 1, :]
    hist_rows = []
    for j in range(CS_NB):
        st = st_ref[j]
        hist_rows.append(jnp.sum(st * w_hist, axis=0, keepdims=True))
        so_ref[j, 0:B_CONV - 2, :] = st[1:B_CONV - 1, :]
        so_ref[j, B_CONV - 2:B_CONV - 1, :] = u[j:j + 1, :]
    acc = jnp.concatenate(hist_rows, axis=0) + w_ref[B_CONV - 1:B_CONV, :] * u
    h_ref[...] = _ln_swish(acc, cb_ref[...], lg_ref[...], lb_ref[...])


def _conv_sample(y, state, w, cb, lg, lb):
    rb0 = SAMP0 // CS_NB
    vec = pl.BlockSpec((1, B_WIDTH), lambda i: (0, 0))
    return pl.pallas_call(
        _conv_sample_body,
        out_shape=[jax.ShapeDtypeStruct((DEC_BATCH, B_WIDTH), BF16),
                   jax.ShapeDtypeStruct((DEC_BATCH, B_CONV - 1, B_WIDTH), F32)],
        grid=(DEC_BATCH // CS_NB,),
        in_specs=[pl.BlockSpec((CS_NB, B_WIDTH), lambda i: (rb0 + i, Y_BU // B_WIDTH)),
                  pl.BlockSpec((CS_NB, B_WIDTH), lambda i: (rb0 + i, Y_BG // B_WIDTH)),
                  pl.BlockSpec((CS_NB, B_CONV - 1, B_WIDTH), lambda i: (i, 0, 0)),
                  pl.BlockSpec((B_CONV, B_WIDTH), lambda i: (0, 0)),
                  vec, vec, vec],
        out_specs=[pl.BlockSpec((CS_NB, B_WIDTH), lambda i: (i, 0)),
                   pl.BlockSpec((CS_NB, B_CONV - 1, B_WIDTH), lambda i: (i, 0, 0))],
        compiler_params=_params("parallel"),
        name="conv_sample",
    )(y, y, state, w, cb, lg, lb)


def _rope_rotate(x, cos_t, sin_t):
    lane = lax.broadcasted_iota(jnp.int32, x.shape, 1)
    swapped = jnp.where(lane < C_ROPE // 2, pltpu.roll(x, 128 - C_ROPE // 2, 1), pltpu.roll(x, C_ROPE // 2, 1))
    return x * cos_t + swapped * sin_t


def _qpost_body(q_ref, cos_ref, sin_ref, gn_ref, gr_ref, wuk_ref, o_ref):
    cos_t, sin_t = cos_ref[...], sin_ref[...]
    gn, gr = gn_ref[...], gr_ref[...]
    for h in range(C_HEADS):
        nope = q_ref[:, h * QH:h * QH + C_NOPE]
        rot = _rope_rotate(q_ref[:, h * QH + C_NOPE:(h + 1) * QH], cos_t, sin_t)
        ms = (jnp.sum(nope * nope, axis=1, keepdims=True) + jnp.sum(rot * rot, axis=1, keepdims=True)) / C_QK
        r = lax.rsqrt(ms + EPS)
        q_lat = jnp.dot((nope * r * gn).astype(BF16), wuk_ref[h], preferred_element_type=F32)
        o_ref[h, :, 0:C_KV_RANK] = q_lat.astype(BF16)
        o_ref[h, :, C_KV_RANK:KW] = (rot * r * gr).astype(BF16)


def _qpost(q, cos_t, sin_t, gain_n, gain_r, wuk_t, *, tm):
    return pl.pallas_call(
        _qpost_body,
        out_shape=jax.ShapeDtypeStruct((C_HEADS, MP, KW), BF16),
        grid=(MP // tm,),
        in_specs=[pl.BlockSpec((tm, C_HEADS * QH), lambda i: (i, 0)),
                  pl.BlockSpec((tm, 128), lambda i: (i, 0)),
                  pl.BlockSpec((tm, 128), lambda i: (i, 0)),
                  pl.BlockSpec((1, 128), lambda i: (0, 0)),
                  pl.BlockSpec((1, 128), lambda i: (0, 0)),
                  pl.BlockSpec((C_HEADS, C_NOPE, C_KV_RANK), lambda i: (0, 0, 0))],
        out_specs=pl.BlockSpec((C_HEADS, tm, KW), lambda i: (0, i, 0)),
        compiler_params=_params("parallel"),
        name="mla_qpost",
    )(q, cos_t, sin_t, gain_n, gain_r, wuk_t)


def _rows_body(ckv_ref, sm_ref, cos_ref, sin_ref, g_ref, wuk_ref, rows_ref, kb_ref, e_ref):
    x = ckv_ref[...]
    c = x * lax.rsqrt(jnp.mean(x * x, axis=-1, keepdims=True) + EPS) * g_ref[...]
    k_r = _rope_rotate(sm_ref[...], cos_ref[...], sin_ref[...])
    k_nope = jnp.dot(c.astype(BF16), wuk_ref[...], preferred_element_type=F32)
    kr_ss = jnp.sum(k_r * k_r, axis=1, keepdims=True)
    lane = lax.broadcasted_iota(jnp.int32, k_r.shape, 1)
    kls = jnp.zeros(k_r.shape, F32)
    for h in range(C_HEADS):
        kn = k_nope[:, h * C_NOPE:(h + 1) * C_NOPE]
        ms = (jnp.sum(kn * kn, axis=1, keepdims=True) + kr_ss) / C_QK
        kls = jnp.where(lane == h, -0.5 * jnp.log(ms + EPS), kls)
    rows_ref[:, 0:C_KV_RANK] = c
    tail = k_r + pltpu.roll(kls, C_ROPE, 1)
    rows_ref[:, C_KV_RANK:C_ROW] = tail[:, 0:C_ROW - C_KV_RANK]
    kb_ref[:, 0:C_KV_RANK] = c.astype(BF16)
    kb_ref[:, C_KV_RANK:KW] = k_r.astype(BF16)
    e_ref[...] = jnp.where(lane < C_HEADS, jnp.exp(kls), 0.0)


def _rows(y, cos_t, sin_t, g_ckv, wuk2, *, tm):
    return pl.pallas_call(
        _rows_body,
        out_shape=[jax.ShapeDtypeStruct((MP, C_ROW), F32),
                   jax.ShapeDtypeStruct((MP, KW), BF16),
                   jax.ShapeDtypeStruct((MP, 128), F32)],
        grid=(MP // tm,),
        in_specs=[pl.BlockSpec((tm, C_KV_RANK), lambda i: (i, Y_CKV // C_KV_RANK)),
                  pl.BlockSpec((tm, 128), lambda i: (i, Y_SMALL // 128)),
                  pl.BlockSpec((tm, 128), lambda i: (i, 0)),
                  pl.BlockSpec((tm, 128), lambda i: (i, 0)),
                  pl.BlockSpec((1, C_KV_RANK), lambda i: (0, 0)),
                  pl.BlockSpec((C_KV_RANK, C_HEADS * C_NOPE), lambda i: (0, 0))],
        out_specs=[pl.BlockSpec((tm, C_ROW), lambda i: (i, 0)),
                   pl.BlockSpec((tm, KW), lambda i: (i, 0)),
                   pl.BlockSpec((tm, 128), lambda i: (i, 0))],
        compiler_params=_params("parallel"),
        name="mla_rows",
    )(y, y, cos_t, sin_t, g_ckv, wuk2)


ATT_TQ = 512


def _attn_finish(p_parts, v_parts, l, wuv):
    o = None
    for p, v in zip(p_parts, v_parts):
        t = jnp.dot(p.astype(BF16), v, preferred_element_type=F32)
        o = t if o is None else o + t
    o = o / l
    return jnp.dot(o.astype(BF16), wuv, preferred_element_type=F32).astype(BF16)


def _attn_real_body(q_ref, k_ref, km_ref, e_ref, em_ref, wuv_ref, o_ref):
    qi = pl.program_id(1)
    q = q_ref[...]
    km = km_ref[...]
    nt = (((1,), (1,)), ((), ()))
    s_m = lax.dot_general(q, km, nt, preferred_element_type=F32) * em_ref[...]
    for j in range(SEQ // ATT_TQ):
        @pl.when(qi == j)
        def _(j=j):
            ext = (j + 1) * ATT_TQ
            kk = k_ref[0:ext, :]
            s = lax.dot_general(q, kk, nt, preferred_element_type=F32) * e_ref[:, 0:ext]
            qpos = j * ATT_TQ + lax.broadcasted_iota(jnp.int32, (ATT_TQ, ext), 0)
            kpos = lax.broadcasted_iota(jnp.int32, (ATT_TQ, ext), 1)
            s = jnp.where(kpos <= qpos, s, -jnp.inf)
            m = jnp.maximum(jnp.max(s, axis=1, keepdims=True), jnp.max(s_m, axis=1, keepdims=True))
            p = jnp.exp(s - m)
            p_m = jnp.exp(s_m - m)
            l = jnp.sum(p, axis=1, keepdims=True) + jnp.sum(p_m, axis=1, keepdims=True)
            o_ref[...] = _attn_finish([p_m, p], [km[:, 0:C_KV_RANK], kk[:, 0:C_KV_RANK]], l, wuv_ref[...])


def _attn_real(qa, kb, et_real, et_meta, wuv):
    meta_blk = META0 // N_META
    nq = SEQ // ATT_TQ
    return pl.pallas_call(
        _attn_real_body,
        out_shape=jax.ShapeDtypeStruct((N_REAL, C_HEADS * C_V), BF16),
        grid=(BATCH, nq, C_HEADS),
        in_specs=[pl.BlockSpec((None, ATT_TQ, KW), lambda b, qi, h: (h, b * nq + qi, 0)),
                  pl.BlockSpec((SEQ, KW), lambda b, qi, h: (b, 0)),
                  pl.BlockSpec((N_META, KW), lambda b, qi, h: (meta_blk + b, 0)),
                  pl.BlockSpec((None, 1, SEQ), lambda b, qi, h: (h, 0, b)),
                  pl.BlockSpec((None, None, 1, N_META), lambda b, qi, h: (h, b, 0, 0)),
                  pl.BlockSpec((None, C_KV_RANK, C_V), lambda b, qi, h: (h, 0, 0))],
        out_specs=pl.BlockSpec((ATT_TQ, C_V), lambda b, qi, h: (b * nq + qi, h)),
        compiler_params=_params("parallel", "parallel", "parallel"),
        name="mla_attn_prompt",
    )(qa, kb, kb, et_real, et_meta, wuv)


def _attn_meta_body(q_ref, km_ref, em_ref, wuv_ref, o_ref):
    km = km_ref[...]
    nt = (((1,), (1,)), ((), ()))
    qpos = lax.broadcasted_iota(jnp.int32, (N_META, N_META), 0)
    kpos = lax.broadcasted_iota(jnp.int32, (N_META, N_META), 1)
    for h in range(C_HEADS):
        s = lax.dot_general(q_ref[h], km, nt, preferred_element_type=F32) * em_ref[h]
        s = jnp.where(kpos <= qpos, s, -jnp.inf)
        m = jnp.max(s, axis=1, keepdims=True)
        p = jnp.exp(s - m)
        l = jnp.sum(p, axis=1, keepdims=True)
        o_ref[:, h * C_V:(h + 1) * C_V] = _attn_finish([p], [km[:, 0:C_KV_RANK]], l, wuv_ref[h])


def _attn_meta(qa, kb, et_meta, wuv):
    meta_blk = META0 // N_META
    return pl.pallas_call(
        _attn_meta_body,
        out_shape=jax.ShapeDtypeStruct((BATCH * N_META, C_HEADS * C_V), BF16),
        grid=(BATCH,),
        in_specs=[pl.BlockSpec((C_HEADS, N_META, KW), lambda b: (0, meta_blk + b, 0)),
                  pl.BlockSpec((N_META, KW), lambda b: (meta_blk + b, 0)),
                  pl.BlockSpec((C_HEADS, None, 1, N_META), lambda b: (0, b, 0, 0)),
                  pl.BlockSpec((C_HEADS, C_KV_RANK, C_V), lambda b: (0, 0, 0))],
        out_specs=pl.BlockSpec((N_META, C_HEADS * C_V), lambda b: (b, 0)),
        compiler_params=_params("parallel"),
        name="mla_attn_meta",
    )(qa, kb, et_meta, wuv)


N_GROUPS = N_PAGES // PAGE_GROUP


def _paged_body(pt_ref, q_ref, kown_ref, eown_ref, cache_ref, o_ref, buf, sem, *, layer):
    step = pl.program_id(0)
    nsteps = pl.num_programs(0)
    units = range(PAGED_SEQS)

    def page_copy(st, g, slot, u, i):
        page = pt_ref[st * PAGED_SEQS + u, g * PAGE_GROUP + i]
        return pltpu.make_async_copy(cache_ref.at[layer, page], buf.at[slot, u, i], sem.at[slot, u])

    def start_group(st, g, slot):
        for u in units:
            for i in range(PAGE_GROUP):
                page_copy(st, g, slot, u, i).start()

    def wait_group(st, g, slot):
        for u in units:
            for i in range(PAGE_GROUP):
                page_copy(st, g, slot, u, i).wait()

    @pl.when(step == 0)
    def _():
        for g in range(PAGE_SLOTS - 1):
            start_group(0, g, g)

    nt = (((1,), (1,)), ((), ()))
    qs, init = [], []
    for u in units:
        q = q_ref[u]
        kown = kown_ref[u].astype(BF16).astype(F32)
        m0 = jnp.sum(q.astype(F32) * kown, axis=1, keepdims=True) * eown_ref[u]
        qs.append(q)
        init.append((m0, jnp.ones((C_HEADS, 1), F32), jnp.broadcast_to(kown[:, 0:C_KV_RANK], (C_HEADS, C_KV_RANK))))

    def attend(u, slot, m, l, acc):
        s_parts = []
        for i in range(PAGE_GROUP):
            pg = buf[slot, u, i]
            s_i = jnp.dot(qs[u], pg.astype(BF16), preferred_element_type=F32)
            s_parts.append(s_i * jnp.exp(pg[C_KV_RANK + C_ROPE:C_ROW, :]))
        s = jnp.concatenate(s_parts, axis=1)
        m_new = jnp.maximum(m, jnp.max(s, axis=1, keepdims=True))
        alpha = jnp.exp(m - m_new)
        p = jnp.exp(s - m_new).astype(BF16)
        l = alpha * l + jnp.sum(p.astype(F32), axis=1, keepdims=True)
        pv = jnp.zeros((C_HEADS, C_KV_RANK), F32)
        for i in range(PAGE_GROUP):
            c_t = buf[slot, u, i, 0:C_KV_RANK, :].astype(BF16)
            pv = pv + lax.dot_general(p[:, i * PAGE_SIZE:(i + 1) * PAGE_SIZE], c_t, nt, preferred_element_type=F32)
        return m_new, l, alpha * acc + pv

    def group(g, carry):
        slot = g % PAGE_SLOTS
        ahead = g + (PAGE_SLOTS - 1)
        ahead_slot = ahead % PAGE_SLOTS

        @pl.when(ahead < N_GROUPS)
        def _():
            start_group(step, ahead, ahead_slot)

        @pl.when(jnp.logical_and(ahead >= N_GROUPS, step + 1 < nsteps))
        def _():
            start_group(step + 1, ahead - N_GROUPS, ahead_slot)

        wait_group(step, g, slot)
        return tuple(attend(u, slot, *carry[u]) for u in units)

    final = lax.fori_loop(0, N_GROUPS, group, tuple(init))
    for u in units:
        _, l, acc = final[u]
        o_ref[u] = (acc / l).astype(BF16)


def _paged_attention(page_table, q_s, k_own, e_own, cache_t, layer):
    grid_spec = pltpu.PrefetchScalarGridSpec(
        num_scalar_prefetch=1,
        grid=(DEC_BATCH // PAGED_SEQS,),
        in_specs=[pl.BlockSpec((PAGED_SEQS, C_HEADS, C_ROW), lambda b, pt: (b, 0, 0)),
                  pl.BlockSpec((PAGED_SEQS, 1, C_ROW), lambda b, pt: (b, 0, 0)),
                  pl.BlockSpec((PAGED_SEQS, C_HEADS, 1), lambda b, pt: (b, 0, 0)),
                  pl.BlockSpec(memory_space=pl.ANY)],
        out_specs=pl.BlockSpec((PAGED_SEQS, C_HEADS, C_KV_RANK), lambda b, pt: (b, 0, 0)),
        scratch_shapes=[pltpu.VMEM((PAGE_SLOTS, PAGED_SEQS, PAGE_GROUP, C_ROW, PAGE_SIZE), F32),
                        pltpu.SemaphoreType.DMA((PAGE_SLOTS, PAGED_SEQS))],
    )
    return pl.pallas_call(
        functools.partial(_paged_body, layer=layer),
        out_shape=jax.ShapeDtypeStruct((DEC_BATCH, C_HEADS, C_KV_RANK), BF16),
        grid_spec=grid_spec,
        compiler_params=_params("arbitrary"),
        name="mla_attn_paged",
    )(page_table, q_s, k_own, e_own, cache_t)


def _uv_body(o_ref, w_ref, out_ref):
    out_ref[...] = jnp.dot(o_ref[...], w_ref[...], preferred_element_type=F32).astype(BF16)


def _uv_sample(o_lat_t, wuv):
    return pl.pallas_call(
        _uv_body,
        out_shape=jax.ShapeDtypeStruct((DEC_BATCH, C_HEADS * C_V), BF16),
        grid=(C_HEADS,),
        in_specs=[pl.BlockSpec((None, DEC_BATCH, C_KV_RANK), lambda h: (h, 0, 0)),
                  pl.BlockSpec((None, C_KV_RANK, C_V), lambda h: (h, 0, 0))],
        out_specs=pl.BlockSpec((DEC_BATCH, C_V), lambda h: (0, h)),
        compiler_params=_params("parallel"),
        name="mla_uv_sample",
    )(o_lat_t, wuv)


FFN_TC = 256
FFN_PAD = 8


def _ffn_act(x0, x1, x2, w):
    return w[0:1, :] * x0 + w[1:2, :] * x1 + w[2:3, :] * x2


def _ffn_prompt_body(gm_ref, um_ref, g_ref, u_ref, wg_ref, wu_ref, am_ref, a_ref, gs, us):
    n = N_META + SEQ
    for src_m, src, dst in ((gm_ref, g_ref, gs), (um_ref, u_ref, us)):
        dst[0:FFN_PAD, :] = jnp.zeros((FFN_PAD, FFN_TC), F32)
        dst[FFN_PAD:FFN_PAD + N_META, :] = src_m[...]
        dst[FFN_PAD + N_META:FFN_PAD + n, :] = src[...]

    def conv(s, w_ref):
        w = w_ref[...]
        return _ffn_act(s[FFN_PAD - 2:FFN_PAD - 2 + n, :], s[FFN_PAD - 1:FFN_PAD - 1 + n, :], s[FFN_PAD:FFN_PAD + n, :], w)

    gate = conv(gs, wg_ref)
    up = conv(us, wu_ref)
    act = (gate * jax.nn.sigmoid(gate) * up).astype(BF16)
    am_ref[...] = act[0:N_META, :]
    a_ref[...] = act[N_META:n, :]


def _ffn_prompt(hup, w_conv):
    meta_blk = META0 // N_META
    nj = D_FF // FFN_TC
    return pl.pallas_call(
        _ffn_prompt_body,
        out_shape=[jax.ShapeDtypeStruct((BATCH * N_META, D_FF), BF16),
                   jax.ShapeDtypeStruct((N_REAL, D_FF), BF16)],
        grid=(BATCH, nj),
        in_specs=[pl.BlockSpec((N_META, FFN_TC), lambda b, j: (meta_blk + b, j)),
                  pl.BlockSpec((N_META, FFN_TC), lambda b, j: (meta_blk + b, nj + j)),
                  pl.BlockSpec((SEQ, FFN_TC), lambda b, j: (b, j)),
                  pl.BlockSpec((SEQ, FFN_TC), lambda b, j: (b, nj + j)),
                  pl.BlockSpec((FFN_CONV, FFN_TC), lambda b, j: (0, j)),
                  pl.BlockSpec((FFN_CONV, FFN_TC), lambda b, j: (0, nj + j))],
        out_specs=[pl.BlockSpec((N_META, FFN_TC), lambda b, j: (b, j)),
                   pl.BlockSpec((SEQ, FFN_TC), lambda b, j: (b, j))],
        scratch_shapes=[pltpu.VMEM((FFN_PAD + N_META + SEQ, FFN_TC), F32),
                        pltpu.VMEM((FFN_PAD + N_META + SEQ, FFN_TC), F32)],
        compiler_params=_params("parallel", "parallel"),
        name="ffn_act_prompt",
    )(hup, hup, hup, hup, w_conv, w_conv)


FFN_TS = 1024


def _ffn_sample_body(g_ref, u_ref, g2_ref, u2_ref, g1_ref, u1_ref, wg_ref, wu_ref, a_ref):
    gate = _ffn_act(g2_ref[...], g1_ref[...], g_ref[...], wg_ref[...])
    up = _ffn_act(u2_ref[...], u1_ref[...], u_ref[...], wu_ref[...])
    a_ref[...] = (gate * jax.nn.sigmoid(gate) * up).astype(BF16)


def _ffn_sample(hup, prev2, prev1, w_conv):
    nj = D_FF // FFN_TS
    rb = SAMP0 // DEC_BATCH

    def st(arr_is_up):
        return pl.BlockSpec((DEC_BATCH, FFN_TS), (lambda j: (0, nj + j)) if arr_is_up else (lambda j: (0, j)))

    return pl.pallas_call(
        _ffn_sample_body,
        out_shape=jax.ShapeDtypeStruct((DEC_BATCH, D_FF), BF16),
        grid=(nj,),
        in_specs=[pl.BlockSpec((DEC_BATCH, FFN_TS), lambda j: (rb, j)),
                  pl.BlockSpec((DEC_BATCH, FFN_TS), lambda j: (rb, nj + j)),
                  st(False), st(True), st(False), st(True),
                  pl.BlockSpec((FFN_CONV, FFN_TS), lambda j: (0, j)),
                  pl.BlockSpec((FFN_CONV, FFN_TS), lambda j: (0, nj + j))],
        out_specs=pl.BlockSpec((DEC_BATCH, FFN_TS), lambda j: (0, j)),
        compiler_params=_params("parallel"),
        name="ffn_act_sample",
    )(hup, hup, prev2, prev2, prev1, prev1, w_conv, w_conv)


def _join_rows(real, meta, samp):
    pad = jnp.zeros((SAMP0 - N_PROMPT, real.shape[1]), real.dtype)
    return jnp.concatenate([real, meta, pad, samp], axis=0)


def _rope_tables():
    pos = np.zeros((MP,), np.float32)
    pos[:N_REAL] = np.tile(N_META + np.arange(SEQ), BATCH)
    pos[META0:N_PROMPT] = np.tile(np.arange(N_META), BATCH)
    pos[SAMP0:] = PAST_LEN
    half = C_ROPE // 2
    inv_freq = ROPE_THETA ** (-jnp.arange(half, dtype=F32) / half)
    ang = jnp.asarray(pos)[:, None] * inv_freq[None, :]
    cos, sin = jnp.cos(ang), jnp.sin(ang)
    zeros = jnp.zeros((MP, 128 - C_ROPE), F32)
    return jnp.concatenate([cos, cos, zeros], axis=1), jnp.concatenate([-sin, sin, zeros], axis=1)


def _prompt_rows(arr, width):
    real = arr[:N_REAL].reshape(BATCH, SEQ, width)
    meta = arr[META0:N_PROMPT].reshape(BATCH, N_META, width)
    return jnp.concatenate([meta, real], axis=1)


def _layer(x, l, cos_t, sin_t, cache_t, page_table, st_c, st_n, st_m, st_conv, st_ffn, p):
    xn = _rms_norm(x, p["g_mix"][l], width=D_MODEL, col_off=0, tm=352, name="norm_mix")
    y = _proj_in(xn, p["w_in_t"], l, tm=1056)

    bias = jnp.concatenate([p["b_i"][l], p["b_f"][l]]).astype(F32)
    gcols = y[:, Y_GATE + LANE_AI:Y_GATE + LANE_AI + 2 * A_HEADS]
    gt_real = gcols[:N_REAL].reshape(BATCH, SEQ // A_CHUNK, A_CHUNK, 2 * A_HEADS).transpose(0, 3, 1, 2)
    gt_meta = gcols[META0:N_PROMPT].reshape(BATCH, N_META, 2 * A_HEADS).transpose(0, 2, 1)[:, :, None, :]
    gnorm = p["g_a_norm"][l].reshape(A_HEADS, 1, A_HEAD_DIM)
    ha_m, ha_r, c_p, n_p, m_p = _mlstm_prompt(y, gt_meta, gt_real, bias, gnorm)
    ys = y[SAMP0:]
    kt = ys[:, Y_AK:Y_AK + A_WIDTH].reshape(DEC_BATCH, A_HEADS, A_HEAD_DIM).transpose(1, 2, 0)
    qt = ys[:, Y_AQ:Y_AQ + A_WIDTH].reshape(DEC_BATCH, A_HEADS, A_HEAD_DIM).transpose(1, 2, 0)
    ha_s, c_s, n_s_t, m_s_t = _mlstm_sample(
        y, kt, qt, st_c, st_n[l].transpose(1, 0, 2), st_m[l].T[:, :, None], bias, gnorm, l)
    ha = _join_rows(ha_r, ha_m, ha_s)
    mixed = _matmul(ha, p["w_a_out"], layer=l, tm=1056, tn=1024, gate=y, gate_off=Y_GA, name="proj_a")

    cb = p["b_b_conv"][l].reshape(1, B_WIDTH)
    lg = p["g_b_ln"][l].reshape(1, B_WIDTH)
    lb = p["b_b_ln"][l].reshape(1, B_WIDTH)
    hb_m, hb_r, conv_p = _conv_prompt(y, p["w_b_conv"][l], cb, lg, lb)
    hb_s, conv_s = _conv_sample(y, st_conv[l], p["w_b_conv"][l], cb, lg, lb)
    hb = _join_rows(hb_r, hb_m, hb_s)
    mixed = _matmul(hb, p["w_b_out"], layer=l, tm=1056, tn=1024, gate=y, gate_off=Y_GB, prev=mixed, name="proj_b")

    w_q = p["w_q_b"][l].reshape(C_Q_RANK, C_HEADS, C_QK)
    w_q = jnp.pad(w_q, ((0, 0), (0, 0), (0, QH - C_QK))).reshape(C_Q_RANK, C_HEADS * QH).astype(BF16)
    cqn = _rms_norm(y, p["g_cq"][l], width=C_Q_RANK, col_off=Y_CQ, tm=1056, name="norm_cq")
    q = _matmul(cqn, w_q, tm=1056, tn=1024, name="proj_q")
    gain = p["g_qn"][l] * (p["g_kn"][l] * C_QK ** -0.5)
    gain_n = gain[:C_NOPE].reshape(1, C_NOPE)
    gain_r = jnp.pad(gain[C_NOPE:], (0, 128 - C_ROPE)).reshape(1, 128)
    w_uk = p["w_uk"][l]
    qa = _qpost(q, cos_t, sin_t, gain_n, gain_r, w_uk.transpose(1, 2, 0).astype(BF16), tm=352)
    rows, kb, e = _rows(y, cos_t, sin_t, p["g_ckv"][l].reshape(1, C_KV_RANK),
                        w_uk.reshape(C_KV_RANK, C_HEADS * C_NOPE).astype(BF16), tm=352)
    wuv = p["w_uv"][l].transpose(1, 0, 2).astype(BF16)
    et = e[:, :C_HEADS].T
    et_real = et[:, None, :N_REAL]
    et_meta = et[:, META0:N_PROMPT].reshape(C_HEADS, BATCH, 1, N_META)
    o_r = _attn_real(qa, kb, et_real, et_meta, wuv)
    o_m = _attn_meta(qa, kb, et_meta, wuv)
    q_s = qa[:, SAMP0:, :C_ROW].transpose(1, 0, 2)
    q_s = q_s.at[:, :, C_KV_RANK + C_ROPE:].set(0)
    o_lat_s = _paged_attention(page_table, q_s, rows[SAMP0:, None, :], e[SAMP0:, :C_HEADS, None], cache_t, l)
    o_s = _uv_sample(o_lat_s.transpose(1, 0, 2), wuv)
    o = _join_rows(o_r, o_m, o_s)
    mixed = _matmul(o, p["w_c_out"], layer=l, tm=1056, tn=512, gate=y, gate_off=Y_GC, prev=mixed,
                    out_dtype=BF16, name="proj_c")

    x = _matmul(mixed, p["w_out"], layer=l, tm=1056, tn=512, resid=x, name="proj_out")

    xn2 = _rms_norm(x, p["g_ffn"][l], width=D_MODEL, col_off=0, tm=352, name="norm_ffn")
    hup = _matmul(xn2, p["w_up"], layer=l, tm=1056, tn=512, name="proj_up")
    act_m, act_r = _ffn_prompt(hup, p["w_ffn_conv"][l])
    act_s = _ffn_sample(hup, st_ffn[l][:, 0, :], st_ffn[l][:, 1, :], p["w_ffn_conv"][l])
    act = _join_rows(act_r, act_m, act_s)
    x = _matmul(act, p["w_down"], layer=l, tm=1056, tn=512, tk=4096, resid=x, name="proj_down")

    hup_s = hup[SAMP0:]
    ffn_p = jnp.stack([hup[b * SEQ + SEQ - (FFN_CONV - 1):(b + 1) * SEQ] for b in range(BATCH)])
    outs = dict(
        rows_p=_prompt_rows(rows, C_ROW), rows_s=rows[SAMP0:, None, :],
        c_p=c_p, c_s=c_s,
        n_p=n_p.reshape(BATCH, A_HEADS, A_HEAD_DIM), n_s=n_s_t.transpose(1, 0, 2),
        m_p=m_p.reshape(BATCH, A_HEADS), m_s=m_s_t[:, :, 0].T,
        conv_p=conv_p, conv_s=conv_s,
        ffn_p=ffn_p,
        ffn_s=jnp.stack([st_ffn[l][:, 1, :], hup_s], axis=1),
    )
    return x, outs


def kernel(x_prompt, x_sample, cache_mla, page_table, state_mlstm_c, state_mlstm_n, state_mlstm_m, state_conv, state_ffn_conv, meta_tokens, g_mix, w_in, b_i, b_f, g_a_norm, w_a_out, w_b_conv, b_b_conv, g_b_ln, b_b_ln, w_b_out, g_cq, w_q_b, g_ckv, w_uk, w_uv, g_qn, g_kn, w_c_out, w_out, g_ffn, w_up, w_ffn_conv, w_down):
    p = dict(g_mix=g_mix, w_in=w_in, b_i=b_i, b_f=b_f, g_a_norm=g_a_norm, w_a_out=w_a_out, w_b_conv=w_b_conv,
             b_b_conv=b_b_conv, g_b_ln=g_b_ln, b_b_ln=b_b_ln, w_b_out=w_b_out, g_cq=g_cq, w_q_b=w_q_b,
             g_ckv=g_ckv, w_uk=w_uk, w_uv=w_uv, g_qn=g_qn, g_kn=g_kn, w_c_out=w_c_out, w_out=w_out,
             g_ffn=g_ffn, w_up=w_up, w_ffn_conv=w_ffn_conv, w_down=w_down)
    cache_t = jnp.swapaxes(cache_mla, 2, 3)
    p["w_in_t"] = jnp.swapaxes(w_in, 1, 2)
    x = _join_rows(x_prompt.reshape(N_REAL, D_MODEL),
                   jnp.tile(meta_tokens.astype(x_prompt.dtype), (BATCH, 1)),
                   x_sample.reshape(DEC_BATCH, D_MODEL))
    cos_t, sin_t = _rope_tables()
    per_layer = []
    for l in range(DEPTH):
        x, outs = _layer(x, l, cos_t, sin_t, cache_t, page_table, state_mlstm_c, state_mlstm_n, state_mlstm_m,
                         state_conv, state_ffn_conv, p)
        per_layer.append(outs)
    stack = lambda name: jnp.stack([o[name] for o in per_layer])
    return (x[:N_REAL].reshape(BATCH, SEQ, D_MODEL), x[SAMP0:].reshape(DEC_BATCH, 1, D_MODEL),
            stack("rows_p"), stack("rows_s"), stack("c_p"), stack("c_s"), stack("n_p"), stack("n_s"),
            stack("m_p"), stack("m_s"), stack("conv_p"), stack("conv_s"), stack("ffn_p"), stack("ffn_s"))
```

```python
import functools

import jax
import jax.numpy as jnp
import numpy as np
from jax import lax
from jax.experimental import pallas as pl
from jax.experimental.pallas import tpu as pltpu

F32 = jnp.float32
BF16 = jnp.bfloat16

D_MODEL = 4096
BATCH = 4
SEQ = 2048
DEPTH = 2
DEC_BATCH = 128
PAST_LEN = 16384
PAGE_SIZE = 128
N_PAGES = PAST_LEN // PAGE_SIZE
N_META = 16
A_HEADS = 4
A_HEAD_DIM = 256
A_WIDTH = A_HEADS * A_HEAD_DIM
A_CHUNK = 64
B_WIDTH = 1024
B_CONV = 31
C_HEADS = 16
C_NOPE = 128
C_ROPE = 64
C_QK = C_NOPE + C_ROPE
C_V = 128
C_Q_RANK = 768
C_KV_RANK = 256
C_ROW = C_KV_RANK + C_ROPE + C_HEADS
ROPE_THETA = 10000.0
D_FF = 8192
FFN_CONV = 3
EPS = 1e-6

N_REAL = BATCH * SEQ
META0 = N_REAL
N_PROMPT = N_REAL + BATCH * N_META
SAMP0 = 8320
MP = SAMP0 + DEC_BATCH

Y_AQ, Y_AK, Y_AV, Y_AO = 0, 1024, 2048, 3072
Y_BU, Y_BG = 4096, 5120
Y_GA, Y_GB, Y_GC = 6144, 10240, 14336
Y_CQ, Y_CKV, Y_SMALL, Y_GATE = 18432, 19200, 19456, 19968
IN_TN = 512
IN_SRC = ([1088 + IN_TN * t for t in range(8)] + [5192 + IN_TN * t for t in range(4)]
          + [7240 + IN_TN * t for t in range(24)] + [0, 512, 1024, 5184])
NY = IN_TN * len(IN_SRC)
LANE_AI, LANE_AF = 0, 4

QH = 256
KW = 384

VMEM_LIMIT = 56 * 1024 * 1024
PAGE_GROUP = 8
PAGED_SEQS = 4
PAGE_SLOTS = 4


def _params(*sem):
    return pltpu.CompilerParams(dimension_semantics=sem, vmem_limit_bytes=VMEM_LIMIT)


def _mm_body(*refs, nk, mode, has_prev):
    x_ref, w_ref = refs[0], refs[1]
    idx = 2
    r_ref = g_ref = p_ref = None
    if mode == "resid":
        r_ref = refs[idx]; idx += 1
    if mode == "gate":
        g_ref = refs[idx]; idx += 1
        if has_prev:
            p_ref = refs[idx]; idx += 1
    o_ref = refs[idx]; idx += 1
    acc_ref = refs[idx] if nk > 1 else None

    def epilogue(acc):
        if mode == "resid":
            acc = r_ref[...] + acc
        elif mode == "gate":
            acc = jax.nn.sigmoid(g_ref[...]) * acc
            if p_ref is not None:
                acc = p_ref[...] + acc
        o_ref[...] = acc.astype(o_ref.dtype)

    part = jnp.dot(x_ref[...], w_ref[...].astype(BF16), preferred_element_type=F32)
    if nk == 1:
        epilogue(part)
    else:
        k = pl.program_id(2)

        @pl.when(k == 0)
        def _():
            acc_ref[...] = part

        @pl.when(jnp.logical_and(k > 0, k < nk - 1))
        def _():
            acc_ref[...] += part

        @pl.when(k == nk - 1)
        def _():
            epilogue(acc_ref[...] + part)


def _matmul(x, w, *, tm, tn, tk=None, out_dtype=F32, resid=None, gate=None, gate_off=0, prev=None, layer=None, name):
    m, kdim = x.shape
    n = w.shape[-1]
    tk = kdim if tk is None else tk
    nk = kdim // tk
    assert m % tm == 0 and n % tn == 0 and kdim % tk == 0 and gate_off % tn == 0
    mode = "resid" if resid is not None else ("gate" if gate is not None else "plain")
    if w.ndim == 3:
        w_spec = pl.BlockSpec((None, tk, tn), lambda i, j, k: (layer, k, j))
    else:
        w_spec = pl.BlockSpec((tk, tn), lambda i, j, k: (k, j))
    in_specs = [pl.BlockSpec((tm, tk), lambda i, j, k: (i, k)), w_spec]
    args = [x, w]
    if resid is not None:
        in_specs.append(pl.BlockSpec((tm, tn), lambda i, j, k: (i, j)))
        args.append(resid)
    if gate is not None:
        goff = gate_off // tn
        in_specs.append(pl.BlockSpec((tm, tn), lambda i, j, k: (i, goff + j)))
        args.append(gate)
        if prev is not None:
            in_specs.append(pl.BlockSpec((tm, tn), lambda i, j, k: (i, j)))
            args.append(prev)
    return pl.pallas_call(
        functools.partial(_mm_body, nk=nk, mode=mode, has_prev=prev is not None),
        out_shape=jax.ShapeDtypeStruct((m, n), out_dtype),
        grid=(m // tm, n // tn, nk),
        in_specs=in_specs,
        out_specs=pl.BlockSpec((tm, tn), lambda i, j, k: (i, j)),
        scratch_shapes=[pltpu.VMEM((tm, tn), F32)] if nk > 1 else [],
        compiler_params=_params("parallel", "parallel", "arbitrary"),
        name=name,
    )(*args)


def _proj_in_body(src_ref, x_ref, w_ref, o_ref):
    del src_ref
    w = w_ref[0].astype(BF16)
    o_ref[...] = lax.dot_general(x_ref[...], w, (((1,), (1,)), ((), ())), preferred_element_type=F32)


def _proj_in(xn, w_in_t, layer, *, tm):
    src = jnp.asarray(IN_SRC, jnp.int32)
    grid_spec = pltpu.PrefetchScalarGridSpec(
        num_scalar_prefetch=1,
        grid=(MP // tm, len(IN_SRC)),
        in_specs=[pl.BlockSpec((tm, D_MODEL), lambda i, j, src: (i, 0)),
                  pl.BlockSpec((pl.Element(1), pl.Element(IN_TN), pl.Element(D_MODEL)),
                               lambda i, j, src: (layer, pl.multiple_of(src[j], 8), 0))],
        out_specs=pl.BlockSpec((tm, IN_TN), lambda i, j, src: (i, j)),
    )
    return pl.pallas_call(
        _proj_in_body,
        out_shape=jax.ShapeDtypeStruct((MP, NY), F32),
        grid_spec=grid_spec,
        compiler_params=_params("parallel", "arbitrary"),
        name="proj_in",
    )(src, xn, w_in_t)


def _rms_body(x_ref, g_ref, o_ref):
    x = x_ref[...]
    ms = jnp.mean(x * x, axis=-1, keepdims=True)
    o_ref[...] = (x * lax.rsqrt(ms + EPS) * g_ref[...]).astype(o_ref.dtype)


def _rms_norm(x, g, *, width, col_off, tm, name):
    m = x.shape[0]
    cblk = col_off // width
    return pl.pallas_call(
        _rms_body,
        out_shape=jax.ShapeDtypeStruct((m, width), BF16),
        grid=(m // tm,),
        in_specs=[pl.BlockSpec((tm, width), lambda i: (i, cblk)),
                  pl.BlockSpec((1, width), lambda i: (0, 0))],
        out_specs=pl.BlockSpec((tm, width), lambda i: (i, 0)),
        compiler_params=_params("parallel"),
        name=name,
    )(x, g.reshape(1, width))


def _lane_pick(g, lane_idx):
    lane = lax.broadcasted_iota(jnp.int32, g.shape, 1)
    return jnp.sum(jnp.where(lane == lane_idx, g, 0.0), axis=1, keepdims=True)


def _head_norm_gate(hh, gn, ao):
    ms = jnp.mean(hh * hh, axis=-1, keepdims=True)
    return (hh * lax.rsqrt(ms + EPS) * gn * jax.nn.sigmoid(ao)).astype(BF16)


def _mlstm_prompt_body(bias_ref, qm_ref, km_ref, vm_ref, aom_ref, gm_ref, irm_ref, frm_ref,
                       q_ref, k_ref, v_ref, ao_ref, g_ref, irr_ref, frr_ref, gn_ref,
                       hm_ref, h_ref, c_ref, n_ref, m_ref):
    h = pl.program_id(1)
    b_i = bias_ref[h]
    b_f = bias_ref[A_HEADS + h]
    gn = gn_ref[...]

    def chunk(length, q, k, v, ao, g, ai_row, af_row, n_prev, m_prev):
        k = k * (A_HEAD_DIM ** -0.5)
        it_col = _lane_pick(g, LANE_AI + h) + b_i
        lf_col = jax.nn.log_sigmoid(_lane_pick(g, LANE_AF + h) + b_f)
        it_row = ai_row + b_i
        lf_row = jax.nn.log_sigmoid(af_row + b_f)
        t_i = lax.broadcasted_iota(jnp.int32, (length, length), 0)
        s_i = lax.broadcasted_iota(jnp.int32, (length, length), 1)
        causal = s_i <= t_i
        b_col = jnp.sum(jnp.where(causal, lf_row, 0.0), axis=1, keepdims=True)
        b_row = jnp.sum(jnp.where(t_i <= s_i, lf_col, 0.0), axis=0, keepdims=True)
        log_d = jnp.where(causal, b_col - b_row + it_row, -jnp.inf)
        log_inter = b_col + m_prev
        m_t = jnp.maximum(log_inter, jnp.max(log_d, axis=1, keepdims=True))
        d = jnp.exp(log_d - m_t)
        w_inter = jnp.exp(log_inter - m_t)
        qb = q.astype(BF16)
        kb = k.astype(BF16)
        vb = v.astype(BF16)
        c_prev = c_ref[...]
        s = lax.dot_general(qb, kb, (((1,), (1,)), ((), ())), preferred_element_type=F32) * d
        num = (w_inter * jnp.dot(qb, c_prev.astype(BF16), preferred_element_type=F32)
               + jnp.dot(s.astype(BF16), vb, preferred_element_type=F32))
        den = w_inter * jnp.sum(q * n_prev, axis=1, keepdims=True) + jnp.sum(s, axis=1, keepdims=True)
        hh = num / jnp.maximum(jnp.abs(den), jnp.exp(-m_t))
        m_new = m_t[length - 1:length, :]
        b_last = b_col[length - 1:length, :]
        w_end = jnp.exp(b_last - b_col + it_col - m_new)
        decay = jnp.exp(b_last + m_prev - m_new)
        kw = k * w_end
        c_ref[...] = decay * c_prev + lax.dot_general(
            kw.astype(BF16), vb, (((0,), (0,)), ((), ())), preferred_element_type=F32)
        n_new = decay * n_prev + jnp.sum(kw, axis=0, keepdims=True)
        return _head_norm_gate(hh, gn, ao), n_new, m_new

    c_ref[...] = jnp.zeros_like(c_ref)
    n0 = jnp.zeros((1, A_HEAD_DIM), F32)
    m0 = jnp.zeros((1, 1), F32)
    out, n1, m1 = chunk(N_META, qm_ref[...], km_ref[...], vm_ref[...], aom_ref[...], gm_ref[...],
                        irm_ref[...], frm_ref[...], n0, m0)
    hm_ref[...] = out

    def step(i, carry):
        n_prev, m_prev = carry
        r0 = pl.multiple_of(i * A_CHUNK, A_CHUNK)
        rows = pl.ds(r0, A_CHUNK)
        out, n_new, m_new = chunk(A_CHUNK, q_ref[rows, :], k_ref[rows, :], v_ref[rows, :], ao_ref[rows, :],
                                  g_ref[rows, :], irr_ref[pl.ds(i, 1), :], frr_ref[pl.ds(i, 1), :],
                                  n_prev, m_prev)
        h_ref[rows, :] = out
        return n_new, m_new

    n_fin, m_fin = lax.fori_loop(0, SEQ // A_CHUNK, step, (n1, m1))
    n_ref[...] = n_fin
    m_ref[...] = m_fin


def _mlstm_prompt(y, gt_meta, gt_real, bias, gnorm):
    meta_blk = META0 // N_META

    def ycol(off):
        return off // A_HEAD_DIM

    def meta_spec(off):
        return pl.BlockSpec((N_META, A_HEAD_DIM), lambda b, h: (meta_blk + b, ycol(off) + h))

    def real_spec(off):
        return pl.BlockSpec((SEQ, A_HEAD_DIM), lambda b, h: (b, ycol(off) + h))

    in_specs = [
        pl.BlockSpec(memory_space=pltpu.SMEM),
        meta_spec(Y_AQ), meta_spec(Y_AK), meta_spec(Y_AV), meta_spec(Y_AO),
        pl.BlockSpec((N_META, 128), lambda b, h: (meta_blk + b, Y_GATE // 128)),
        pl.BlockSpec((None, None, 1, N_META), lambda b, h: (b, h, 0, 0)),
        pl.BlockSpec((None, None, 1, N_META), lambda b, h: (b, A_HEADS + h, 0, 0)),
        real_spec(Y_AQ), real_spec(Y_AK), real_spec(Y_AV), real_spec(Y_AO),
        pl.BlockSpec((SEQ, 128), lambda b, h: (b, Y_GATE // 128)),
        pl.BlockSpec((None, None, SEQ // A_CHUNK, A_CHUNK), lambda b, h: (b, h, 0, 0)),
        pl.BlockSpec((None, None, SEQ // A_CHUNK, A_CHUNK), lambda b, h: (b, A_HEADS + h, 0, 0)),
        pl.BlockSpec((None, 1, A_HEAD_DIM), lambda b, h: (h, 0, 0)),
    ]
    out_shape = [
        jax.ShapeDtypeStruct((BATCH * N_META, A_WIDTH), BF16),
        jax.ShapeDtypeStruct((N_REAL, A_WIDTH), BF16),
        jax.ShapeDtypeStruct((BATCH, A_HEADS, A_HEAD_DIM, A_HEAD_DIM), F32),
        jax.ShapeDtypeStruct((BATCH, A_HEADS, 1, A_HEAD_DIM), F32),
        jax.ShapeDtypeStruct((BATCH, A_HEADS, 1, 1), F32),
    ]
    out_specs = [
        pl.BlockSpec((N_META, A_HEAD_DIM), lambda b, h: (b, h)),
        pl.BlockSpec((SEQ, A_HEAD_DIM), lambda b, h: (b, h)),
        pl.BlockSpec((None, None, A_HEAD_DIM, A_HEAD_DIM), lambda b, h: (b, h, 0, 0)),
        pl.BlockSpec((None, None, 1, A_HEAD_DIM), lambda b, h: (b, h, 0, 0)),
        pl.BlockSpec((None, None, 1, 1), lambda b, h: (b, h, 0, 0)),
    ]
    return pl.pallas_call(
        _mlstm_prompt_body,
        out_shape=out_shape,
        grid=(BATCH, A_HEADS),
        in_specs=in_specs,
        out_specs=out_specs,
        compiler_params=_params("parallel", "parallel"),
        name="mlstm_prompt",
    )(bias, y, y, y, y, y, gt_meta, gt_meta, y, y, y, y, y, gt_real, gt_real, gnorm)


MS_NB = 8


def _mlstm_sample_body(bias_ref, q_ref, k_ref, v_ref, ao_ref, g_ref, kt_ref, qt_ref, c_ref, n_ref, m_ref, gn_ref,
                       h_ref, co_ref, no_ref, mo_ref):
    h = pl.program_id(0)
    bt = pl.program_id(1)
    b_i = bias_ref[h]
    b_f = bias_ref[A_HEADS + h]
    q = q_ref[...]
    k = k_ref[...] * (A_HEAD_DIM ** -0.5)
    v = v_ref[...]
    g = g_ref[...]
    it = _lane_pick(g, LANE_AI + h) + b_i
    lf = jax.nn.log_sigmoid(_lane_pick(g, LANE_AF + h) + b_f)
    m_prev = m_ref[...]
    n_prev = n_ref[...]
    log_inter = lf + m_prev
    m_t = jnp.maximum(log_inter, it)
    d = jnp.exp(it - m_t)
    w_inter = jnp.exp(log_inter - m_t)
    s = jnp.sum(q * k, axis=1, keepdims=True) * d
    den = w_inter * jnp.sum(q * n_prev, axis=1, keepdims=True) + s
    wv = d * v
    lane = lax.broadcasted_iota(jnp.int32, (A_HEAD_DIM, DEC_BATCH), 1)
    kt = kt_ref[...] * (A_HEAD_DIM ** -0.5)
    qt = qt_ref[...]
    qc_rows = []
    for j in range(MS_NB):
        sel = lane == bt * MS_NB + j
        k_col = jnp.sum(jnp.where(sel, kt, 0.0), axis=1, keepdims=True)
        q_col = jnp.sum(jnp.where(sel, qt, 0.0), axis=1, keepdims=True)
        c_prev = c_ref[j]
        qc_rows.append(jnp.sum(q_col * c_prev, axis=0, keepdims=True))
        co_ref[j] = w_inter[j:j + 1, :] * c_prev + k_col * wv[j:j + 1, :]
    qc = jnp.concatenate(qc_rows, axis=0)
    num = w_inter * qc + s * v
    hh = num / jnp.maximum(jnp.abs(den), jnp.exp(-m_t))
    h_ref[...] = _head_norm_gate(hh, gn_ref[...], ao_ref[...])
    no_ref[...] = w_inter * n_prev + d * k
    mo_ref[...] = m_t


def _mlstm_sample(y, kt, qt, c_state, n_state_t, m_state_t, bias, gnorm, layer):
    rb0 = SAMP0 // MS_NB

    def yspec(off):
        return pl.BlockSpec((MS_NB, A_HEAD_DIM), lambda h, bt: (rb0 + bt, off // A_HEAD_DIM + h))

    in_specs = [
        pl.BlockSpec(memory_space=pltpu.SMEM),
        yspec(Y_AQ), yspec(Y_AK), yspec(Y_AV), yspec(Y_AO),
        pl.BlockSpec((MS_NB, 128), lambda h, bt: (rb0 + bt, Y_GATE // 128)),
        pl.BlockSpec((None, A_HEAD_DIM, DEC_BATCH), lambda h, bt: (h, 0, 0)),
        pl.BlockSpec((None, A_HEAD_DIM, DEC_BATCH), lambda h, bt: (h, 0, 0)),
        pl.BlockSpec((None, MS_NB, None, A_HEAD_DIM, A_HEAD_DIM), lambda h, bt: (layer, bt, h, 0, 0)),
        pl.BlockSpec((None, MS_NB, A_HEAD_DIM), lambda h, bt: (h, bt, 0)),
        pl.BlockSpec((None, MS_NB, 1), lambda h, bt: (h, bt, 0)),
        pl.BlockSpec((None, 1, A_HEAD_DIM), lambda h, bt: (h, 0, 0)),
    ]
    out_shape = [
        jax.ShapeDtypeStruct((DEC_BATCH, A_WIDTH), BF16),
        jax.ShapeDtypeStruct((DEC_BATCH, A_HEADS, A_HEAD_DIM, A_HEAD_DIM), F32),
        jax.ShapeDtypeStruct((A_HEADS, DEC_BATCH, A_HEAD_DIM), F32),
        jax.ShapeDtypeStruct((A_HEADS, DEC_BATCH, 1), F32),
    ]
    out_specs = [
        pl.BlockSpec((MS_NB, A_HEAD_DIM), lambda h, bt: (bt, h)),
        pl.BlockSpec((MS_NB, None, A_HEAD_DIM, A_HEAD_DIM), lambda h, bt: (bt, h, 0, 0)),
        pl.BlockSpec((None, MS_NB, A_HEAD_DIM), lambda h, bt: (h, bt, 0)),
        pl.BlockSpec((None, MS_NB, 1), lambda h, bt: (h, bt, 0)),
    ]
    return pl.pallas_call(
        _mlstm_sample_body,
        out_shape=out_shape,
        grid=(A_HEADS, DEC_BATCH // MS_NB),
        in_specs=in_specs,
        out_specs=out_specs,
        compiler_params=_params("parallel", "parallel"),
        name="mlstm_sample",
    )(bias, y, y, y, y, y, kt, qt, c_state, n_state_t, m_state_t, gnorm)


CONV_PAD = 32
CONV_ROWS = 32


def _ln_swish(u, bias, g, b):
    u = u + bias
    mu = jnp.mean(u, axis=-1, keepdims=True)
    var = jnp.mean(jnp.square(u - mu), axis=-1, keepdims=True)
    y = (u - mu) * lax.rsqrt(var + EPS) * g + b
    return (y * jax.nn.sigmoid(y)).astype(BF16)


def _conv_prompt_body(bum_ref, bgm_ref, bu_ref, bg_ref, w_ref, cb_ref, lg_ref, lb_ref,
                      hm_ref, h_ref, st_ref, u_s):
    seq0 = CONV_PAD + N_META
    u_s[0:CONV_PAD, :] = jnp.zeros((CONV_PAD, B_WIDTH), F32)
    u_s[CONV_PAD:seq0, :] = bum_ref[...] * jax.nn.sigmoid(bgm_ref[...])

    def glu(i, _):
        rows = pl.ds(pl.multiple_of(i * 256, 256), 256)
        u_s[pl.ds(pl.multiple_of(seq0 + i * 256, 16), 256), :] = bu_ref[rows, :] * jax.nn.sigmoid(bg_ref[rows, :])
        return 0

    lax.fori_loop(0, SEQ // 256, glu, 0)
    bias, lg, lb = cb_ref[...], lg_ref[...], lb_ref[...]

    def conv_rows(start, nrows):
        win = u_s[pl.ds(start, nrows + CONV_PAD), :]
        acc = jnp.zeros((nrows, B_WIDTH), F32)
        for b in range(8):
            taps = range(b, B_CONV, 8)
            shifted = win[2 + b:2 + b + nrows + taps[-1] - b, :]
            for k in taps:
                acc = acc + w_ref[k:k + 1, :] * shifted[k - b:k - b + nrows, :]
        return _ln_swish(acc, bias, lg, lb)

    hm_ref[...] = conv_rows(0, N_META)

    def step(i, _):
        r0 = pl.multiple_of(i * CONV_ROWS, CONV_ROWS)
        h_ref[pl.ds(r0, CONV_ROWS), :] = conv_rows(pl.multiple_of(N_META + r0, 8), CONV_ROWS)
        return 0

    lax.fori_loop(0, SEQ // CONV_ROWS, step, 0)
    st_ref[...] = u_s[CONV_PAD + N_META + SEQ - (B_CONV - 1):CONV_PAD + N_META + SEQ, :]


def _conv_prompt(y, w, cb, lg, lb):
    meta_blk = META0 // N_META
    cu, cg = Y_BU // B_WIDTH, Y_BG // B_WIDTH
    vec = pl.BlockSpec((1, B_WIDTH), lambda b: (0, 0))
    return pl.pallas_call(
        _conv_prompt_body,
        out_shape=[jax.ShapeDtypeStruct((BATCH * N_META, B_WIDTH), BF16),
                   jax.ShapeDtypeStruct((N_REAL, B_WIDTH), BF16),
                   jax.ShapeDtypeStruct((BATCH, B_CONV - 1, B_WIDTH), F32)],
        grid=(BATCH,),
        in_specs=[pl.BlockSpec((N_META, B_WIDTH), lambda b: (meta_blk + b, cu)),
                  pl.BlockSpec((N_META, B_WIDTH), lambda b: (meta_blk + b, cg)),
                  pl.BlockSpec((SEQ, B_WIDTH), lambda b: (b, cu)),
                  pl.BlockSpec((SEQ, B_WIDTH), lambda b: (b, cg)),
                  pl.BlockSpec((B_CONV, B_WIDTH), lambda b: (0, 0)),
                  vec, vec, vec],
        out_specs=[pl.BlockSpec((N_META, B_WIDTH), lambda b: (b, 0)),
                   pl.BlockSpec((SEQ, B_WIDTH), lambda b: (b, 0)),
                   pl.BlockSpec((None, B_CONV - 1, B_WIDTH), lambda b: (b, 0, 0))],
        scratch_shapes=[pltpu.VMEM((CONV_PAD + N_META + SEQ, B_WIDTH), F32)],
        compiler_params=_params("parallel"),
        name="conv_prompt",
    )(y, y, y, y, w, cb, lg, lb)


CS_NB = 8


def _conv_sample_body(bu_ref, bg_ref, st_ref, w_ref, cb_ref, lg_ref, lb_ref, h_ref, so_ref):
    u = bu_ref[...] * jax.nn.sigmoid(bg_ref[...])
    w_hist = w_ref[0:B_CONV - 1, :]
    hist_rows = []
    for j in range(CS_NB):
        st = st_ref[j]
        hist_rows.append(jnp.sum(st * w_hist, axis=0, keepdims=True))
        so_ref[j, 0:B_CONV - 2, :] = st[1:B_CONV - 1, :]
        so_ref[j, B_CONV - 2:B_CONV - 1, :] = u[j:j + 1, :]
    acc = jnp.concatenate(hist_rows, axis=0) + w_ref[B_CONV - 1:B_CONV, :] * u
    h_ref[...] = _ln_swish(acc, cb_ref[...], lg_ref[...], lb_ref[...])


def _conv_sample(y, state, w, cb, lg, lb):
    rb0 = SAMP0 // CS_NB
    vec = pl.BlockSpec((1, B_WIDTH), lambda i: (0, 0))
    return pl.pallas_call(
        _conv_sample_body,
        out_shape=[jax.ShapeDtypeStruct((DEC_BATCH, B_WIDTH), BF16),
                   jax.ShapeDtypeStruct((DEC_BATCH, B_CONV - 1, B_WIDTH), F32)],
        grid=(DEC_BATCH // CS_NB,),
        in_specs=[pl.BlockSpec((CS_NB, B_WIDTH), lambda i: (rb0 + i, Y_BU // B_WIDTH)),
                  pl.BlockSpec((CS_NB, B_WIDTH), lambda i: (rb0 + i, Y_BG // B_WIDTH)),
                  pl.BlockSpec((CS_NB, B_CONV - 1, B_WIDTH), lambda i: (i, 0, 0)),
                  pl.BlockSpec((B_CONV, B_WIDTH), lambda i: (0, 0)),
                  vec, vec, vec],
        out_specs=[pl.BlockSpec((CS_NB, B_WIDTH), lambda i: (i, 0)),
                   pl.BlockSpec((CS_NB, B_CONV - 1, B_WIDTH), lambda i: (i, 0, 0))],
        compiler_params=_params("parallel"),
        name="conv_sample",
    )(y, y, state, w, cb, lg, lb)


def _rope_rotate(x, cos_t, sin_t):
    lane = lax.broadcasted_iota(jnp.int32, x.shape, 1)
    swapped = jnp.where(lane < C_ROPE // 2, pltpu.roll(x, 128 - C_ROPE // 2, 1), pltpu.roll(x, C_ROPE // 2, 1))
    return x * cos_t + swapped * sin_t


def _qpost_body(q_ref, cos_ref, sin_ref, gn_ref, gr_ref, wuk_ref, o_ref):
    cos_t, sin_t = cos_ref[...], sin_ref[...]
    gn, gr = gn_ref[...], gr_ref[...]
    for h in range(C_HEADS):
        nope = q_ref[:, h * QH:h * QH + C_NOPE]
        rot = _rope_rotate(q_ref[:, h * QH + C_NOPE:(h + 1) * QH], cos_t, sin_t)
        ms = (jnp.sum(nope * nope, axis=1, keepdims=True) + jnp.sum(rot * rot, axis=1, keepdims=True)) / C_QK
        r = lax.rsqrt(ms + EPS)
        q_lat = jnp.dot((nope * r * gn).astype(BF16), wuk_ref[h], preferred_element_type=F32)
        o_ref[h, :, 0:C_KV_RANK] = q_lat.astype(BF16)
        o_ref[h, :, C_KV_RANK:KW] = (rot * r * gr).astype(BF16)


def _qpost(q, cos_t, sin_t, gain_n, gain_r, wuk_t, *, tm):
    return pl.pallas_call(
        _qpost_body,
        out_shape=jax.ShapeDtypeStruct((C_HEADS, MP, KW), BF16),
        grid=(MP // tm,),
        in_specs=[pl.BlockSpec((tm, C_HEADS * QH), lambda i: (i, 0)),
                  pl.BlockSpec((tm, 128), lambda i: (i, 0)),
                  pl.BlockSpec((tm, 128), lambda i: (i, 0)),
                  pl.BlockSpec((1, 128), lambda i: (0, 0)),
                  pl.BlockSpec((1, 128), lambda i: (0, 0)),
                  pl.BlockSpec((C_HEADS, C_NOPE, C_KV_RANK), lambda i: (0, 0, 0))],
        out_specs=pl.BlockSpec((C_HEADS, tm, KW), lambda i: (0, i, 0)),
        compiler_params=_params("parallel"),
        name="mla_qpost",
    )(q, cos_t, sin_t, gain_n, gain_r, wuk_t)


def _rows_body(ckv_ref, sm_ref, cos_ref, sin_ref, g_ref, wuk_ref, rows_ref, kb_ref, e_ref):
    x = ckv_ref[...]
    c = x * lax.rsqrt(jnp.mean(x * x, axis=-1, keepdims=True) + EPS) * g_ref[...]
    k_r = _rope_rotate(sm_ref[...], cos_ref[...], sin_ref[...])
    k_nope = jnp.dot(c.astype(BF16), wuk_ref[...], preferred_element_type=F32)
    kr_ss = jnp.sum(k_r * k_r, axis=1, keepdims=True)
    lane = lax.broadcasted_iota(jnp.int32, k_r.shape, 1)
    kls = jnp.zeros(k_r.shape, F32)
    for h in range(C_HEADS):
        kn = k_nope[:, h * C_NOPE:(h + 1) * C_NOPE]
        ms = (jnp.sum(kn * kn, axis=1, keepdims=True) + kr_ss) / C_QK
        kls = jnp.where(lane == h, -0.5 * jnp.log(ms + EPS), kls)
    rows_ref[:, 0:C_KV_RANK] = c
    tail = k_r + pltpu.roll(kls, C_ROPE, 1)
    rows_ref[:, C_KV_RANK:C_ROW] = tail[:, 0:C_ROW - C_KV_RANK]
    kb_ref[:, 0:C_KV_RANK] = c.astype(BF16)
    kb_ref[:, C_KV_RANK:KW] = k_r.astype(BF16)
    e_ref[...] = jnp.where(lane < C_HEADS, jnp.exp(kls), 0.0)


def _rows(y, cos_t, sin_t, g_ckv, wuk2, *, tm):
    return pl.pallas_call(
        _rows_body,
        out_shape=[jax.ShapeDtypeStruct((MP, C_ROW), F32),
                   jax.ShapeDtypeStruct((MP, KW), BF16),
                   jax.ShapeDtypeStruct((MP, 128), F32)],
        grid=(MP // tm,),
        in_specs=[pl.BlockSpec((tm, C_KV_RANK), lambda i: (i, Y_CKV // C_KV_RANK)),
                  pl.BlockSpec((tm, 128), lambda i: (i, Y_SMALL // 128)),
                  pl.BlockSpec((tm, 128), lambda i: (i, 0)),
                  pl.BlockSpec((tm, 128), lambda i: (i, 0)),
                  pl.BlockSpec((1, C_KV_RANK), lambda i: (0, 0)),
                  pl.BlockSpec((C_KV_RANK, C_HEADS * C_NOPE), lambda i: (0, 0))],
        out_specs=[pl.BlockSpec((tm, C_ROW), lambda i: (i, 0)),
                   pl.BlockSpec((tm, KW), lambda i: (i, 0)),
                   pl.BlockSpec((tm, 128), lambda i: (i, 0))],
        compiler_params=_params("parallel"),
        name="mla_rows",
    )(y, y, cos_t, sin_t, g_ckv, wuk2)


ATT_TQ = 512


def _attn_finish(p_parts, v_parts, l, wuv):
    o = None
    for p, v in zip(p_parts, v_parts):
        t = jnp.dot(p.astype(BF16), v, preferred_element_type=F32)
        o = t if o is None else o + t
    o = o / l
    return jnp.dot(o.astype(BF16), wuv, preferred_element_type=F32).astype(BF16)


def _attn_real_body(q_ref, k_ref, km_ref, e_ref, em_ref, wuv_ref, o_ref):
    qi = pl.program_id(1)
    q = q_ref[...]
    km = km_ref[...]
    nt = (((1,), (1,)), ((), ()))
    s_m = lax.dot_general(q, km, nt, preferred_element_type=F32) * em_ref[...]
    m0 = jnp.max(s_m, axis=1, keepdims=True)
    p_m = jnp.exp(s_m - m0)
    l0 = jnp.sum(p_m, axis=1, keepdims=True)
    acc0 = jnp.dot(p_m.astype(BF16), km[:, 0:C_KV_RANK], preferred_element_type=F32)
    rpos = lax.broadcasted_iota(jnp.int32, (ATT_TQ, ATT_TQ), 0)
    cpos = lax.broadcasted_iota(jnp.int32, (ATT_TQ, ATT_TQ), 1)
    for j in range(SEQ // ATT_TQ):
        @pl.when(qi == j)
        def _(j=j):
            m, l, acc = m0, l0, acc0
            for c in range(j + 1):
                kk = k_ref[c * ATT_TQ:(c + 1) * ATT_TQ, :]
                s = lax.dot_general(q, kk, nt, preferred_element_type=F32) * e_ref[:, c * ATT_TQ:(c + 1) * ATT_TQ]
                if c == j:
                    s = jnp.where(cpos <= rpos, s, -jnp.inf)
                m_new = jnp.maximum(m, jnp.max(s, axis=1, keepdims=True))
                alpha = jnp.exp(m - m_new)
                p = jnp.exp(s - m_new)
                l = alpha * l + jnp.sum(p, axis=1, keepdims=True)
                acc = alpha * acc + jnp.dot(p.astype(BF16), kk[:, 0:C_KV_RANK], preferred_element_type=F32)
                m = m_new
            o = acc / l
            o_ref[...] = jnp.dot(o.astype(BF16), wuv_ref[...], preferred_element_type=F32).astype(BF16)


def _attn_real(qa, kb, et_real, et_meta, wuv):
    meta_blk = META0 // N_META
    nq = SEQ // ATT_TQ
    return pl.pallas_call(
        _attn_real_body,
        out_shape=jax.ShapeDtypeStruct((N_REAL, C_HEADS * C_V), BF16),
        grid=(BATCH, nq, C_HEADS),
        in_specs=[pl.BlockSpec((None, ATT_TQ, KW), lambda b, qi, h: (h, b * nq + qi, 0)),
                  pl.BlockSpec((SEQ, KW), lambda b, qi, h: (b, 0)),
                  pl.BlockSpec((N_META, KW), lambda b, qi, h: (meta_blk + b, 0)),
                  pl.BlockSpec((None, 1, SEQ), lambda b, qi, h: (h, 0, b)),
                  pl.BlockSpec((None, None, 1, N_META), lambda b, qi, h: (h, b, 0, 0)),
                  pl.BlockSpec((None, C_KV_RANK, C_V), lambda b, qi, h: (h, 0, 0))],
        out_specs=pl.BlockSpec((ATT_TQ, C_V), lambda b, qi, h: (b * nq + qi, h)),
        compiler_params=_params("parallel", "parallel", "parallel"),
        name="mla_attn_prompt",
    )(qa, kb, kb, et_real, et_meta, wuv)


def _attn_meta_body(q_ref, km_ref, em_ref, wuv_ref, o_ref):
    km = km_ref[...]
    nt = (((1,), (1,)), ((), ()))
    qpos = lax.broadcasted_iota(jnp.int32, (N_META, N_META), 0)
    kpos = lax.broadcasted_iota(jnp.int32, (N_META, N_META), 1)
    for h in range(C_HEADS):
        s = lax.dot_general(q_ref[h], km, nt, preferred_element_type=F32) * em_ref[h]
        s = jnp.where(kpos <= qpos, s, -jnp.inf)
        m = jnp.max(s, axis=1, keepdims=True)
        p = jnp.exp(s - m)
        l = jnp.sum(p, axis=1, keepdims=True)
        o_ref[:, h * C_V:(h + 1) * C_V] = _attn_finish([p], [km[:, 0:C_KV_RANK]], l, wuv_ref[h])


def _attn_meta(qa, kb, et_meta, wuv):
    meta_blk = META0 // N_META
    return pl.pallas_call(
        _attn_meta_body,
        out_shape=jax.ShapeDtypeStruct((BATCH * N_META, C_HEADS * C_V), BF16),
        grid=(BATCH,),
        in_specs=[pl.BlockSpec((C_HEADS, N_META, KW), lambda b: (0, meta_blk + b, 0)),
                  pl.BlockSpec((N_META, KW), lambda b: (meta_blk + b, 0)),
                  pl.BlockSpec((C_HEADS, None, 1, N_META), lambda b: (0, b, 0, 0)),
                  pl.BlockSpec((C_HEADS, C_KV_RANK, C_V), lambda b: (0, 0, 0))],
        out_specs=pl.BlockSpec((N_META, C_HEADS * C_V), lambda b: (b, 0)),
        compiler_params=_params("parallel"),
        name="mla_attn_meta",
    )(qa, kb, et_meta, wuv)


N_GROUPS = N_PAGES // PAGE_GROUP


def _paged_body(pt_ref, q_ref, kown_ref, eown_ref, cache_ref, o_ref, buf, sem, *, layer):
    step = pl.program_id(0)
    nsteps = pl.num_programs(0)
    units = range(PAGED_SEQS)

    def page_copy(st, g, slot, u, i):
        page = pt_ref[st * PAGED_SEQS + u, g * PAGE_GROUP + i]
        return pltpu.make_async_copy(cache_ref.at[layer, page], buf.at[slot, u, i], sem.at[slot, u])

    def start_group(st, g, slot):
        for u in units:
            for i in range(PAGE_GROUP):
                page_copy(st, g, slot, u, i).start()

    def wait_group(st, g, slot):
        for u in units:
            for i in range(PAGE_GROUP):
                page_copy(st, g, slot, u, i).wait()

    @pl.when(step == 0)
    def _():
        for g in range(PAGE_SLOTS - 1):
            start_group(0, g, g)

    nt = (((1,), (1,)), ((), ()))
    qs, init = [], []
    for u in units:
        q = q_ref[u]
        kown = kown_ref[u].astype(BF16).astype(F32)
        m0 = jnp.sum(q.astype(F32) * kown, axis=1, keepdims=True) * eown_ref[u]
        qs.append(q)
        init.append((m0, jnp.ones((C_HEADS, 1), F32), jnp.broadcast_to(kown[:, 0:C_KV_RANK], (C_HEADS, C_KV_RANK))))

    def attend(u, slot, m, l, acc):
        s_parts = []
        for i in range(PAGE_GROUP):
            pg = buf[slot, u, i]
            s_i = jnp.dot(qs[u], pg.astype(BF16), preferred_element_type=F32)
            s_parts.append(s_i * jnp.exp(pg[C_KV_RANK + C_ROPE:C_ROW, :]))
        s = jnp.concatenate(s_parts, axis=1)
        m_new = jnp.maximum(m, jnp.max(s, axis=1, keepdims=True))
        alpha = jnp.exp(m - m_new)
        p = jnp.exp(s - m_new).astype(BF16)
        l = alpha * l + jnp.sum(p.astype(F32), axis=1, keepdims=True)
        pv = jnp.zeros((C_HEADS, C_KV_RANK), F32)
        for i in range(PAGE_GROUP):
            c_t = buf[slot, u, i, 0:C_KV_RANK, :].astype(BF16)
            pv = pv + lax.dot_general(p[:, i * PAGE_SIZE:(i + 1) * PAGE_SIZE], c_t, nt, preferred_element_type=F32)
        return m_new, l, alpha * acc + pv

    def group(g, carry):
        slot = g % PAGE_SLOTS
        ahead = g + (PAGE_SLOTS - 1)
        ahead_slot = ahead % PAGE_SLOTS

        @pl.when(ahead < N_GROUPS)
        def _():
            start_group(step, ahead, ahead_slot)

        @pl.when(jnp.logical_and(ahead >= N_GROUPS, step + 1 < nsteps))
        def _():
            start_group(step + 1, ahead - N_GROUPS, ahead_slot)

        wait_group(step, g, slot)
        return tuple(attend(u, slot, *carry[u]) for u in units)

    final = lax.fori_loop(0, N_GROUPS, group, tuple(init))
    for u in units:
        _, l, acc = final[u]
        o_ref[u] = (acc / l).astype(BF16)


def _paged_attention(page_table, q_s, k_own, e_own, cache_t, layer):
    grid_spec = pltpu.PrefetchScalarGridSpec(
        num_scalar_prefetch=1,
        grid=(DEC_BATCH // PAGED_SEQS,),
        in_specs=[pl.BlockSpec((PAGED_SEQS, C_HEADS, C_ROW), lambda b, pt: (b, 0, 0)),
                  pl.BlockSpec((PAGED_SEQS, 1, C_ROW), lambda b, pt: (b, 0, 0)),
                  pl.BlockSpec((PAGED_SEQS, C_HEADS, 1), lambda b, pt: (b, 0, 0)),
                  pl.BlockSpec(memory_space=pl.ANY)],
        out_specs=pl.BlockSpec((PAGED_SEQS, C_HEADS, C_KV_RANK), lambda b, pt: (b, 0, 0)),
        scratch_shapes=[pltpu.VMEM((PAGE_SLOTS, PAGED_SEQS, PAGE_GROUP, C_ROW, PAGE_SIZE), F32),
                        pltpu.SemaphoreType.DMA((PAGE_SLOTS, PAGED_SEQS))],
    )
    return pl.pallas_call(
        functools.partial(_paged_body, layer=layer),
        out_shape=jax.ShapeDtypeStruct((DEC_BATCH, C_HEADS, C_KV_RANK), BF16),
        grid_spec=grid_spec,
        compiler_params=_params("arbitrary"),
        name="mla_attn_paged",
    )(page_table, q_s, k_own, e_own, cache_t)


def _uv_body(o_ref, w_ref, out_ref):
    out_ref[...] = jnp.dot(o_ref[...], w_ref[...], preferred_element_type=F32).astype(BF16)


def _uv_sample(o_lat_t, wuv):
    return pl.pallas_call(
        _uv_body,
        out_shape=jax.ShapeDtypeStruct((DEC_BATCH, C_HEADS * C_V), BF16),
        grid=(C_HEADS,),
        in_specs=[pl.BlockSpec((None, DEC_BATCH, C_KV_RANK), lambda h: (h, 0, 0)),
                  pl.BlockSpec((None, C_KV_RANK, C_V), lambda h: (h, 0, 0))],
        out_specs=pl.BlockSpec((DEC_BATCH, C_V), lambda h: (0, h)),
        compiler_params=_params("parallel"),
        name="mla_uv_sample",
    )(o_lat_t, wuv)


FFN_TC = 256
FFN_PAD = 8


def _ffn_act(x0, x1, x2, w):
    return w[0:1, :] * x0 + w[1:2, :] * x1 + w[2:3, :] * x2


def _ffn_prompt_body(gm_ref, um_ref, g_ref, u_ref, wg_ref, wu_ref, am_ref, a_ref, gs, us):
    n = N_META + SEQ
    for src_m, src, dst in ((gm_ref, g_ref, gs), (um_ref, u_ref, us)):
        dst[0:FFN_PAD, :] = jnp.zeros((FFN_PAD, FFN_TC), F32)
        dst[FFN_PAD:FFN_PAD + N_META, :] = src_m[...]
        dst[FFN_PAD + N_META:FFN_PAD + n, :] = src[...]

    def conv(s, w_ref):
        w = w_ref[...]
        return _ffn_act(s[FFN_PAD - 2:FFN_PAD - 2 + n, :], s[FFN_PAD - 1:FFN_PAD - 1 + n, :], s[FFN_PAD:FFN_PAD + n, :], w)

    gate = conv(gs, wg_ref)
    up = conv(us, wu_ref)
    act = (gate * jax.nn.sigmoid(gate) * up).astype(BF16)
    am_ref[...] = act[0:N_META, :]
    a_ref[...] = act[N_META:n, :]


def _ffn_prompt(hup, w_conv):
    meta_blk = META0 // N_META
    nj = D_FF // FFN_TC
    return pl.pallas_call(
        _ffn_prompt_body,
        out_shape=[jax.ShapeDtypeStruct((BATCH * N_META, D_FF), BF16),
                   jax.ShapeDtypeStruct((N_REAL, D_FF), BF16)],
        grid=(BATCH, nj),
        in_specs=[pl.BlockSpec((N_META, FFN_TC), lambda b, j: (meta_blk + b, j)),
                  pl.BlockSpec((N_META, FFN_TC), lambda b, j: (meta_blk + b, nj + j)),
                  pl.BlockSpec((SEQ, FFN_TC), lambda b, j: (b, j)),
                  pl.BlockSpec((SEQ, FFN_TC), lambda b, j: (b, nj + j)),
                  pl.BlockSpec((FFN_CONV, FFN_TC), lambda b, j: (0, j)),
                  pl.BlockSpec((FFN_CONV, FFN_TC), lambda b, j: (0, nj + j))],
        out_specs=[pl.BlockSpec((N_META, FFN_TC), lambda b, j: (b, j)),
                   pl.BlockSpec((SEQ, FFN_TC), lambda b, j: (b, j))],
        scratch_shapes=[pltpu.VMEM((FFN_PAD + N_META + SEQ, FFN_TC), F32),
                        pltpu.VMEM((FFN_PAD + N_META + SEQ, FFN_TC), F32)],
        compiler_params=_params("parallel", "parallel"),
        name="ffn_act_prompt",
    )(hup, hup, hup, hup, w_conv, w_conv)


FFN_TS = 1024


def _ffn_sample_body(g_ref, u_ref, g2_ref, u2_ref, g1_ref, u1_ref, wg_ref, wu_ref, a_ref):
    gate = _ffn_act(g2_ref[...], g1_ref[...], g_ref[...], wg_ref[...])
    up = _ffn_act(u2_ref[...], u1_ref[...], u_ref[...], wu_ref[...])
    a_ref[...] = (gate * jax.nn.sigmoid(gate) * up).astype(BF16)


def _ffn_sample(hup, prev2, prev1, w_conv):
    nj = D_FF // FFN_TS
    rb = SAMP0 // DEC_BATCH

    def st(arr_is_up):
        return pl.BlockSpec((DEC_BATCH, FFN_TS), (lambda j: (0, nj + j)) if arr_is_up else (lambda j: (0, j)))

    return pl.pallas_call(
        _ffn_sample_body,
        out_shape=jax.ShapeDtypeStruct((DEC_BATCH, D_FF), BF16),
        grid=(nj,),
        in_specs=[pl.BlockSpec((DEC_BATCH, FFN_TS), lambda j: (rb, j)),
                  pl.BlockSpec((DEC_BATCH, FFN_TS), lambda j: (rb, nj + j)),
                  st(False), st(True), st(False), st(True),
                  pl.BlockSpec((FFN_CONV, FFN_TS), lambda j: (0, j)),
                  pl.BlockSpec((FFN_CONV, FFN_TS), lambda j: (0, nj + j))],
        out_specs=pl.BlockSpec((DEC_BATCH, FFN_TS), lambda j: (0, j)),
        compiler_params=_params("parallel"),
        name="ffn_act_sample",
    )(hup, hup, prev2, prev2, prev1, prev1, w_conv, w_conv)


def _join_rows(real, meta, samp):
    pad = jnp.zeros((SAMP0 - N_PROMPT, real.shape[1]), real.dtype)
    return jnp.concatenate([real, meta, pad, samp], axis=0)


def _rope_tables():
    pos = np.zeros((MP,), np.float32)
    pos[:N_REAL] = np.tile(N_META + np.arange(SEQ), BATCH)
    pos[META0:N_PROMPT] = np.tile(np.arange(N_META), BATCH)
    pos[SAMP0:] = PAST_LEN
    half = C_ROPE // 2
    inv_freq = ROPE_THETA ** (-jnp.arange(half, dtype=F32) / half)
    ang = jnp.asarray(pos)[:, None] * inv_freq[None, :]
    cos, sin = jnp.cos(ang), jnp.sin(ang)
    zeros = jnp.zeros((MP, 128 - C_ROPE), F32)
    return jnp.concatenate([cos, cos, zeros], axis=1), jnp.concatenate([-sin, sin, zeros], axis=1)


def _prompt_rows(arr, width):
    real = arr[:N_REAL].reshape(BATCH, SEQ, width)
    meta = arr[META0:N_PROMPT].reshape(BATCH, N_META, width)
    return jnp.concatenate([meta, real], axis=1)


def _layer(x, l, cos_t, sin_t, cache_t, page_table, st_c, st_n, st_m, st_conv, st_ffn, p):
    xn = _rms_norm(x, p["g_mix"][l], width=D_MODEL, col_off=0, tm=352, name="norm_mix")
    y = _proj_in(xn, p["w_in_t"], l, tm=1056)

    bias = jnp.concatenate([p["b_i"][l], p["b_f"][l]]).astype(F32)
    gcols = y[:, Y_GATE + LANE_AI:Y_GATE + LANE_AI + 2 * A_HEADS]
    gt_real = gcols[:N_REAL].reshape(BATCH, SEQ // A_CHUNK, A_CHUNK, 2 * A_HEADS).transpose(0, 3, 1, 2)
    gt_meta = gcols[META0:N_PROMPT].reshape(BATCH, N_META, 2 * A_HEADS).transpose(0, 2, 1)[:, :, None, :]
    gnorm = p["g_a_norm"][l].reshape(A_HEADS, 1, A_HEAD_DIM)
    ha_m, ha_r, c_p, n_p, m_p = _mlstm_prompt(y, gt_meta, gt_real, bias, gnorm)
    ys = y[SAMP0:]
    kt = ys[:, Y_AK:Y_AK + A_WIDTH].reshape(DEC_BATCH, A_HEADS, A_HEAD_DIM).transpose(1, 2, 0)
    qt = ys[:, Y_AQ:Y_AQ + A_WIDTH].reshape(DEC_BATCH, A_HEADS, A_HEAD_DIM).transpose(1, 2, 0)
    ha_s, c_s, n_s_t, m_s_t = _mlstm_sample(
        y, kt, qt, st_c, st_n[l].transpose(1, 0, 2), st_m[l].T[:, :, None], bias, gnorm, l)
    ha = _join_rows(ha_r, ha_m, ha_s)
    mixed = _matmul(ha, p["w_a_out"], layer=l, tm=1056, tn=1024, gate=y, gate_off=Y_GA, name="proj_a")

    cb = p["b_b_conv"][l].reshape(1, B_WIDTH)
    lg = p["g_b_ln"][l].reshape(1, B_WIDTH)
    lb = p["b_b_ln"][l].reshape(1, B_WIDTH)
    hb_m, hb_r, conv_p = _conv_prompt(y, p["w_b_conv"][l], cb, lg, lb)
    hb_s, conv_s = _conv_sample(y, st_conv[l], p["w_b_conv"][l], cb, lg, lb)
    hb = _join_rows(hb_r, hb_m, hb_s)
    mixed = _matmul(hb, p["w_b_out"], layer=l, tm=1056, tn=1024, gate=y, gate_off=Y_GB, prev=mixed, name="proj_b")

    w_q = p["w_q_b"][l].reshape(C_Q_RANK, C_HEADS, C_QK)
    w_q = jnp.pad(w_q, ((0, 0), (0, 0), (0, QH - C_QK))).reshape(C_Q_RANK, C_HEADS * QH).astype(BF16)
    cqn = _rms_norm(y, p["g_cq"][l], width=C_Q_RANK, col_off=Y_CQ, tm=1056, name="norm_cq")
    q = _matmul(cqn, w_q, tm=1056, tn=1024, name="proj_q")
    gain = p["g_qn"][l] * (p["g_kn"][l] * C_QK ** -0.5)
    gain_n = gain[:C_NOPE].reshape(1, C_NOPE)
    gain_r = jnp.pad(gain[C_NOPE:], (0, 128 - C_ROPE)).reshape(1, 128)
    w_uk = p["w_uk"][l]
    qa = _qpost(q, cos_t, sin_t, gain_n, gain_r, w_uk.transpose(1, 2, 0).astype(BF16), tm=352)
    rows, kb, e = _rows(y, cos_t, sin_t, p["g_ckv"][l].reshape(1, C_KV_RANK),
                        w_uk.reshape(C_KV_RANK, C_HEADS * C_NOPE).astype(BF16), tm=352)
    wuv = p["w_uv"][l].transpose(1, 0, 2).astype(BF16)
    et = e[:, :C_HEADS].T
    et_real = et[:, None, :N_REAL]
    et_meta = et[:, META0:N_PROMPT].reshape(C_HEADS, BATCH, 1, N_META)
    o_r = _attn_real(qa, kb, et_real, et_meta, wuv)
    o_m = _attn_meta(qa, kb, et_meta, wuv)
    q_s = qa[:, SAMP0:, :C_ROW].transpose(1, 0, 2)
    q_s = q_s.at[:, :, C_KV_RANK + C_ROPE:].set(0)
    o_lat_s = _paged_attention(page_table, q_s, rows[SAMP0:, None, :], e[SAMP0:, :C_HEADS, None], cache_t, l)
    o_s = _uv_sample(o_lat_s.transpose(1, 0, 2), wuv)
    o = _join_rows(o_r, o_m, o_s)
    mixed = _matmul(o, p["w_c_out"], layer=l, tm=1056, tn=512, gate=y, gate_off=Y_GC, prev=mixed,
                    out_dtype=BF16, name="proj_c")

    x = _matmul(mixed, p["w_out"], layer=l, tm=1056, tn=512, resid=x, name="proj_out")

    xn2 = _rms_norm(x, p["g_ffn"][l], width=D_MODEL, col_off=0, tm=352, name="norm_ffn")
    hup = _matmul(xn2, p["w_up"], layer=l, tm=1056, tn=512, name="proj_up")
    act_m, act_r = _ffn_prompt(hup, p["w_ffn_conv"][l])
    act_s = _ffn_sample(hup, st_ffn[l][:, 0, :], st_ffn[l][:, 1, :], p["w_ffn_conv"][l])
    act = _join_rows(act_r, act_m, act_s)
    x = _matmul(act, p["w_down"], layer=l, tm=1056, tn=512, tk=4096, resid=x, name="proj_down")

    hup_s = hup[SAMP0:]
    ffn_p = jnp.stack([hup[b * SEQ + SEQ - (FFN_CONV - 1):(b + 1) * SEQ] for b in range(BATCH)])
    outs = dict(
        rows_p=_prompt_rows(rows, C_ROW), rows_s=rows[SAMP0:, None, :],
        c_p=c_p, c_s=c_s,
        n_p=n_p.reshape(BATCH, A_HEADS, A_HEAD_DIM), n_s=n_s_t.transpose(1, 0, 2),
        m_p=m_p.reshape(BATCH, A_HEADS), m_s=m_s_t[:, :, 0].T,
        conv_p=conv_p, conv_s=conv_s,
        ffn_p=ffn_p,
        ffn_s=jnp.stack([st_ffn[l][:, 1, :], hup_s], axis=1),
    )
    return x, outs


def kernel(x_prompt, x_sample, cache_mla, page_table, state_mlstm_c, state_mlstm_n, state_mlstm_m, state_conv, state_ffn_conv, meta_tokens, g_mix, w_in, b_i, b_f, g_a_norm, w_a_out, w_b_conv, b_b_conv, g_b_ln, b_b_ln, w_b_out, g_cq, w_q_b, g_ckv, w_uk, w_uv, g_qn, g_kn, w_c_out, w_out, g_ffn, w_up, w_ffn_conv, w_down):
    p = dict(g_mix=g_mix, w_in=w_in, b_i=b_i, b_f=b_f, g_a_norm=g_a_norm, w_a_out=w_a_out, w_b_conv=w_b_conv,
             b_b_conv=b_b_conv, g_b_ln=g_b_ln, b_b_ln=b_b_ln, w_b_out=w_b_out, g_cq=g_cq, w_q_b=w_q_b,
             g_ckv=g_ckv, w_uk=w_uk, w_uv=w_uv, g_qn=g_qn, g_kn=g_kn, w_c_out=w_c_out, w_out=w_out,
             g_ffn=g_ffn, w_up=w_up, w_ffn_conv=w_ffn_conv, w_down=w_down)
    cache_t = jnp.swapaxes(cache_mla, 2, 3)
    p["w_in_t"] = jnp.swapaxes(w_in, 1, 2)
    x = _join_rows(x_prompt.reshape(N_REAL, D_MODEL),
                   jnp.tile(meta_tokens.astype(x_prompt.dtype), (BATCH, 1)),
                   x_sample.reshape(DEC_BATCH, D_MODEL))
    cos_t, sin_t = _rope_tables()
    per_layer = []
    for l in range(DEPTH):
        x, outs = _layer(x, l, cos_t, sin_t, cache_t, page_table, state_mlstm_c, state_mlstm_n, state_mlstm_m,
                         state_conv, state_ffn_conv, p)
        per_layer.append(outs)
    stack = lambda name: jnp.stack([o[name] for o in per_layer])
    return (x[:N_REAL].reshape(BATCH, SEQ, D_MODEL), x[SAMP0:].reshape(DEC_BATCH, 1, D_MODEL),
            stack("rows_p"), stack("rows_s"), stack("c_p"), stack("c_s"), stack("n_p"), stack("n_s"),
            stack("m_p"), stack("m_s"), stack("conv_p"), stack("conv_s"), stack("ffn_p"), stack("ffn_s"))
```
